```python
import jax, jax.numpy as jnp
from jax import lax
import numpy as np

D_MODEL = 1024
BATCH = 4
SEQ = 8192
DEPTH = 4

N_A = DEPTH // 2
N_B = DEPTH - N_A
N_MEM = 256
HEAD_DIM = 64
RWKV_HEADS = 12
RWKV_DIM = RWKV_HEADS * HEAD_DIM
MEM_HEADS = 4
MEM_DIM = MEM_HEADS * HEAD_DIM
MIX_WIDTH = RWKV_DIM + MEM_DIM
DECAY_LORA = 64
AAA_LORA = 64
MV_LORA = 32
GATE_LORA = 128
RWKV_IN = 3 * RWKV_DIM + DECAY_LORA + AAA_LORA + GATE_LORA
A_IN = RWKV_IN + MEM_DIM
RWKV_SPLITS = (RWKV_DIM, 2 * RWKV_DIM, 3 * RWKV_DIM, 3 * RWKV_DIM + DECAY_LORA,
               3 * RWKV_DIM + DECAY_LORA + AAA_LORA)
MLA_HEADS = 12
QK_NOPE = 64
QK_ROPE = 32
V_HEAD = 64
Q_LORA = 512
KV_LORA = 256
MLA_DIM = MLA_HEADS * V_HEAD
B_IN = Q_LORA + MEM_DIM
FFN_HIDDEN = ((8 * D_MODEL + 3 * 256 - 1) // (3 * 256)) * 256
ROPE_BASE = 10000.0
NORM_EPS = 1e-6
LNX_EPS = 64e-5
Q_BLOCK = 128

kernel_name = 'yoco_rwkv7_mla_memory_trunk'


def rms_norm(x, g, eps=NORM_EPS):
    xf = x.astype(jnp.float32)
    y = xf * lax.rsqrt(jnp.mean(xf * xf, axis=-1, keepdims=True) + eps)
    return (y * g.astype(jnp.float32)).astype(x.dtype)


def token_shift(t):
    return jnp.pad(t, ((0, 0), (1, 0), (0, 0)))[:, :-1]


def l2_normalize(t):
    tf = t.astype(jnp.float32)
    n = jnp.sqrt(jnp.sum(tf * tf, axis=-1, keepdims=True))
    return (tf / jnp.maximum(n, 1e-12)).astype(t.dtype)


def head_group_norm(y, g, b):
    yf = y.astype(jnp.float32)
    mu = jnp.mean(yf, axis=-1, keepdims=True)
    var = jnp.mean(jnp.square(yf - mu), axis=-1, keepdims=True)
    out = (yf - mu) * lax.rsqrt(var + LNX_EPS)
    out = out * g.reshape(RWKV_HEADS, HEAD_DIM).astype(jnp.float32) + b.reshape(RWKV_HEADS, HEAD_DIM).astype(jnp.float32)
    return out.astype(y.dtype)


def swiglu(h, w_gu, w_down):
    gate, up = jnp.split(h @ w_gu, 2, axis=-1)
    return (jax.nn.silu(gate) * up) @ w_down


def rope_tables(positions):
    half = QK_ROPE // 2
    inv_freq = ROPE_BASE ** (-jnp.arange(half, dtype=jnp.float32) / half)
    ang = positions.astype(jnp.float32)[..., None] * inv_freq
    return jnp.cos(ang), jnp.sin(ang)


def rope(t, cos, sin):
    tf = t.astype(jnp.float32)
    t1, t2 = jnp.split(tf, 2, axis=-1)
    return jnp.concatenate([t1 * cos - t2 * sin, t2 * cos + t1 * sin], axis=-1).astype(t.dtype)


def rwkv7_scan(r, decay, k, v, a, b):
    dtype = r.dtype
    bsz, _, nh, n = r.shape
    seqs = tuple(jnp.moveaxis(t.astype(jnp.float32), 1, 0) for t in (r, decay, k, v, a, b))

    def step(state, inp):
        r_t, w_t, k_t, v_t, a_t, b_t = inp
        sa = jnp.einsum('bhvk,bhk->bhv', state, a_t)
        state = (state * w_t[:, :, None, :] + sa[..., None] * b_t[:, :, None, :]
                 + v_t[..., None] * k_t[:, :, None, :])
        return state, jnp.einsum('bhvk,bhk->bhv', state, r_t)

    s0 = jnp.zeros((bsz, nh, n, n), jnp.float32)
    _, ys = lax.scan(step, s0, seqs)
    return jnp.moveaxis(ys, 0, 1).astype(dtype)


def memory_attention(q, mem_n, w_kv):
    bsz, n_mem, _ = mem_n.shape
    mkv = mem_n @ w_kv
    mk = mkv[..., :MEM_DIM].reshape(bsz, n_mem, MEM_HEADS, HEAD_DIM)
    mv = mkv[..., MEM_DIM:].reshape(bsz, n_mem, MEM_HEADS, HEAD_DIM)
    s = jnp.einsum('bshd,bmhd->bhsm', q, mk, preferred_element_type=jnp.float32) * (HEAD_DIM ** -0.5)
    p = jax.nn.softmax(s, axis=-1).astype(mv.dtype)
    o = jnp.einsum('bhsm,bmhd->bshd', p, mv)
    return o.reshape(q.shape[0], q.shape[1], MEM_DIM)


def causal_mla_attention(q_nope, q_rope, k_nope, k_rope, v):
    bsz, s_len, nh, _ = q_nope.shape
    nb = s_len // Q_BLOCK
    key_pos = jnp.arange(s_len)
    scale = (QK_NOPE + QK_ROPE) ** -0.5

    def to_blocks(t):
        return jnp.moveaxis(t.reshape((bsz, nb, Q_BLOCK) + t.shape[2:]), 1, 0)

    def block(args):
        qn, qr, blk = args
        s = (jnp.einsum('bqhd,bkhd->bhqk', qn, k_nope, preferred_element_type=jnp.float32)
             + jnp.einsum('bqhd,bkd->bhqk', qr, k_rope, preferred_element_type=jnp.float32)) * scale
        q_pos = blk * Q_BLOCK + jnp.arange(Q_BLOCK)
        s = jnp.where(key_pos[None, :] <= q_pos[:, None], s, -jnp.inf)
        p = jax.nn.softmax(s, axis=-1).astype(v.dtype)
        return jnp.einsum('bhqk,bkhd->bqhd', p, v)

    out = lax.map(block, (to_blocks(q_nope), to_blocks(q_rope), jnp.arange(nb)))
    return jnp.moveaxis(out, 0, 1).reshape(bsz, s_len, nh * v.shape[-1])


def setup_inputs(seed: int = 0) -> dict:
    key = jax.random.key(seed)
    keys = iter(jax.random.split(key, 64))

    def nrm(shape, scale):
        return jax.random.normal(next(keys), shape, jnp.float32) * scale

    def gain(shape):
        return 1.0 + nrm(shape, 0.02)

    def unif(shape, lo, hi):
        return jax.random.uniform(next(keys), shape, jnp.float32, lo, hi)

    nv = max(N_A - 1, 0)
    qk_up = MLA_HEADS * (QK_NOPE + QK_ROPE)
    kv_up = MLA_HEADS * (QK_NOPE + V_HEAD)
    return {
        'x': nrm((BATCH, SEQ, D_MODEL), 1.0),
        'mem': nrm((BATCH, N_MEM, D_MODEL), 1.0),
        'positions': (jnp.arange(SEQ, dtype=jnp.int32)[None, :]
                      + jax.random.randint(next(keys), (BATCH, 1), 0, 1024, jnp.int32)),
        'mem_norm_g': gain((D_MODEL,)),
        'a_norm1_g': gain((N_A, D_MODEL)),
        'a_w_in': nrm((N_A, D_MODEL, A_IN), D_MODEL ** -0.5),
        'a_shift_mu': unif((N_A, RWKV_IN), 0.0, 1.0),
        'a_decay_up': nrm((N_A, DECAY_LORA, RWKV_DIM), 0.5 * DECAY_LORA ** -0.5),
        'a_decay_bias': unif((N_A, RWKV_DIM), -6.0, 0.5),
        'a_aaa_up': nrm((N_A, AAA_LORA, RWKV_DIM), AAA_LORA ** -0.5),
        'a_aaa_bias': nrm((N_A, RWKV_DIM), 0.1),
        'a_gate_up': nrm((N_A, GATE_LORA, RWKV_DIM), GATE_LORA ** -0.5),
        'a_k_k': 0.85 + nrm((N_A, RWKV_DIM), 0.05),
        'a_k_a': 1.0 + nrm((N_A, RWKV_DIM), 0.05),
        'a_r_k': nrm((N_A, RWKV_HEADS, HEAD_DIM), 0.1),
        'a_lnx_g': gain((N_A, RWKV_DIM)),
        'a_lnx_b': nrm((N_A, RWKV_DIM), 0.02),
        'a_mem_kv': nrm((N_A, D_MODEL, 2 * MEM_DIM), D_MODEL ** -0.5),
        'a_w_out': nrm((N_A, MIX_WIDTH, D_MODEL), MIX_WIDTH ** -0.5),
        'a_norm2_g': gain((N_A, D_MODEL)),
        'a_ffn_gu': nrm((N_A, D_MODEL, 2 * FFN_HIDDEN), D_MODEL ** -0.5),
        'a_ffn_down': nrm((N_A, FFN_HIDDEN, D_MODEL), FFN_HIDDEN ** -0.5),
        'vres_mu': unif((nv, D_MODEL), 0.0, 1.0),
        'vres_down': nrm((nv, D_MODEL, MV_LORA), D_MODEL ** -0.5),
        'vres_up': nrm((nv, MV_LORA, RWKV_DIM), MV_LORA ** -0.5),
        'vres_bias': nrm((nv, RWKV_DIM), 0.1),
        'kv_norm_g': gain((D_MODEL,)),
        'kv_w_down': nrm((D_MODEL, KV_LORA + QK_ROPE), D_MODEL ** -0.5),
        'kv_latent_g': gain((KV_LORA,)),
        'kv_w_up': nrm((KV_LORA, kv_up), KV_LORA ** -0.5),
        'b_norm1_g': gain((N_B, D_MODEL)),
        'b_w_in': nrm((N_B, D_MODEL, B_IN), D_MODEL ** -0.5),
        'b_q_norm_g': gain((N_B, Q_LORA)),
        'b_q_up': nrm((N_B, Q_LORA, qk_up), Q_LORA ** -0.5),
        'b_mem_kv': nrm((N_B, D_MODEL, 2 * MEM_DIM), D_MODEL ** -0.5),
        'b_w_out': nrm((N_B, MIX_WIDTH, D_MODEL), MIX_WIDTH ** -0.5),
        'b_norm2_g': gain((N_B, D_MODEL)),
        'b_ffn_gu': nrm((N_B, D_MODEL, 2 * FFN_HIDDEN), D_MODEL ** -0.5),
        'b_ffn_down': nrm((N_B, FFN_HIDDEN, D_MODEL), FFN_HIDDEN ** -0.5),
        'final_norm_g': gain((D_MODEL,)),
    }


def reference(x, mem, positions, mem_norm_g,
              a_norm1_g, a_w_in, a_shift_mu, a_decay_up, a_decay_bias, a_aaa_up, a_aaa_bias,
              a_gate_up, a_k_k, a_k_a, a_r_k, a_lnx_g, a_lnx_b, a_mem_kv, a_w_out, a_norm2_g,
              a_ffn_gu, a_ffn_down,
              vres_mu, vres_down, vres_up, vres_bias,
              kv_norm_g, kv_w_down, kv_latent_g, kv_w_up,
              b_norm1_g, b_w_in, b_q_norm_g, b_q_up, b_mem_kv, b_w_out, b_norm2_g,
              b_ffn_gu, b_ffn_down,
              final_norm_g):
    bsz, s_len, _ = x.shape

    def heads(t, nh):
        return t.reshape(bsz, s_len, nh, HEAD_DIM)

    mem_n = rms_norm(mem, mem_norm_g)
    cos, sin = rope_tables(positions)
    v_first = None
    k_nope = k_rope = v_mla = None

    for layer in range(DEPTH):
        if layer < N_A:
            i = layer
            h = rms_norm(x, a_norm1_g[i])
            proj = h @ a_w_in[i]
            p_tm, q_mem = proj[..., :RWKV_IN], proj[..., RWKV_IN:]
            p_tm = p_tm + (token_shift(p_tm) - p_tm) * a_shift_mu[i]
            r, k, v, d_lo, a_lo, g_lo = jnp.split(p_tm, RWKV_SPLITS, axis=-1)
            log_w = -jax.nn.softplus(-(a_decay_bias[i] + jnp.tanh(d_lo) @ a_decay_up[i])) - 0.5
            decay = jnp.exp(-jnp.exp(log_w.astype(jnp.float32)))
            lr = jax.nn.sigmoid(a_aaa_bias[i] + a_lo @ a_aaa_up[i])
            gate = jax.nn.sigmoid(g_lo) @ a_gate_up[i]
            if i == 0:
                v_first = v
            else:
                hv = h + (token_shift(h) - h) * vres_mu[i - 1]
                v = v + (v_first - v) * jax.nn.sigmoid(
                    vres_bias[i - 1] + (hv @ vres_down[i - 1]) @ vres_up[i - 1])
            kk = l2_normalize(heads(k * a_k_k[i], RWKV_HEADS))
            k = k * (1.0 + (lr - 1.0) * a_k_a[i])
            rh, kh, vh, lrh = (heads(t, RWKV_HEADS) for t in (r, k, v, lr))
            y = rwkv7_scan(rh, heads(decay, RWKV_HEADS), kh, vh, -kk, kk * lrh)
            y = head_group_norm(y, a_lnx_g[i], a_lnx_b[i])
            y = y + jnp.sum(rh * kh * a_r_k[i], axis=-1, keepdims=True) * vh
            y_mix = y.reshape(bsz, s_len, RWKV_DIM) * gate
            m = memory_attention(heads(q_mem, MEM_HEADS), mem_n, a_mem_kv[i])
            x = x + jnp.concatenate([y_mix, m], axis=-1) @ a_w_out[i]
            x = x + swiglu(rms_norm(x, a_norm2_g[i]), a_ffn_gu[i], a_ffn_down[i])
        else:
            j = layer - N_A
            if j == 0:
                hk = rms_norm(x, kv_norm_g)
                ckr = hk @ kv_w_down
                c_kv = rms_norm(ckr[..., :KV_LORA], kv_latent_g)
                k_rope = rope(ckr[..., KV_LORA:], cos, sin)
                kv = (c_kv @ kv_w_up).reshape(bsz, s_len, MLA_HEADS, QK_NOPE + V_HEAD)
                k_nope, v_mla = kv[..., :QK_NOPE], kv[..., QK_NOPE:]
            h = rms_norm(x, b_norm1_g[j])
            proj = h @ b_w_in[j]
            c_q = rms_norm(proj[..., :Q_LORA], b_q_norm_g[j])
            q_mem = proj[..., Q_LORA:]
            q = (c_q @ b_q_up[j]).reshape(bsz, s_len, MLA_HEADS, QK_NOPE + QK_ROPE)
            q_nope = q[..., :QK_NOPE]
            q_rope = rope(q[..., QK_NOPE:], cos[:, :, None, :], sin[:, :, None, :])
            o = causal_mla_attention(q_nope, q_rope, k_nope, k_rope, v_mla)
            m = memory_attention(heads(q_mem, MEM_HEADS), mem_n, b_mem_kv[j])
            x = x + jnp.concatenate([o, m], axis=-1) @ b_w_out[j]
            x = x + swiglu(rms_norm(x, b_norm2_g[j]), b_ffn_gu[j], b_ffn_down[j])

    return rms_norm(x, final_norm_g)
```

```python
import functools

import numpy as np
import jax
import jax.numpy as jnp
from jax import lax
from jax.experimental import pallas as pl
from jax.experimental.pallas import tpu as pltpu

F32 = jnp.float32
BF16 = jnp.bfloat16

HEAD = 64
CHUNK = 64
GROUP = 4
GL = GROUP * HEAD
NORM_EPS = 1e-6
LNX_EPS = 64e-5
QK_NOPE = 64
QK_ROPE = 32
MLA_PAD = 128
VMEM_LIMIT = 56 * 1024 * 1024
TILES = dict(tm=512, tc=512, tq=512, tk=512)


def _dot(a, b):
    return jnp.dot(a.astype(BF16), b.astype(BF16), preferred_element_type=F32)


def _dot_nt(a, b):
    return lax.dot_general(a.astype(BF16), b.astype(BF16), (((1,), (1,)), ((), ())),
                           preferred_element_type=F32)


def _dot_tn(a, b):
    return lax.dot_general(a.astype(BF16), b.astype(BF16), (((0,), (0,)), ((), ())),
                           preferred_element_type=F32)


def _split(x, terms):
    out = []
    for _ in range(terms):
        p = x.astype(BF16)
        out.append(p)
        x = x - p.astype(F32)
    return out


def _dot_ones_rhs(x, ones_bf16, terms=2):
    acc = None
    for p in _split(x, terms):
        d = jnp.dot(p, ones_bf16, preferred_element_type=F32)
        acc = d if acc is None else acc + d
    return acc


def _dot_ones_lhs(ones_bf16, x, terms=3):
    acc = None
    for p in _split(x, terms):
        d = jnp.dot(ones_bf16, p, preferred_element_type=F32)
        acc = d if acc is None else acc + d
    return acc


def _rms(x, g, eps=NORM_EPS):
    ms = jnp.mean(x * x, axis=-1, keepdims=True)
    return x * lax.rsqrt(ms + eps) * g


def _sigmoid(x):
    return 1.0 / (1.0 + jnp.exp(-x))


def _shift_rows(x, carry_ref):
    tm = x.shape[0]
    first = lax.broadcasted_iota(jnp.int32, (tm, 1), 0) == 0
    prev = jnp.where(first, carry_ref[0:1, :], pltpu.roll(x, 1, 0))
    carry_ref[0:1, :] = x[tm - 1:tm, :]
    return prev


def _const_spec(shape):
    nd = len(shape)
    return pl.BlockSpec(shape, lambda *_: (0,) * nd, pipeline_mode=pl.Buffered(1))


def _params(sem):
    return pltpu.CompilerParams(dimension_semantics=sem, vmem_limit_bytes=VMEM_LIMIT)


def _tile(n, pref):
    t = min(n, pref)
    assert n % t == 0, (n, t)
    return t


def _memkv_kernel(mem_ref, g_ref, w_ref, o_ref):
    mn = _rms(mem_ref[0], g_ref[...])
    o_ref[0] = _dot(mn, w_ref[...]).astype(o_ref.dtype)


def _memkv(mem, g, w_all):
    bsz, n_mem, d = mem.shape
    n = w_all.shape[1]
    return pl.pallas_call(
        _memkv_kernel,
        grid=(bsz,),
        in_specs=[pl.BlockSpec((1, n_mem, d), lambda b: (b, 0, 0)),
                  _const_spec((1, d)), _const_spec((d, n))],
        out_specs=pl.BlockSpec((1, n_mem, n), lambda b: (b, 0, 0)),
        out_shape=jax.ShapeDtypeStruct((bsz, n_mem, n), BF16),
        compiler_params=_params(("parallel",)),
        name="memkv",
    )(mem, g, w_all)


def _a_pre_kernel(has_vres, rdim, rwkv_in, *refs):
    it = iter(refs)
    x_ref = next(it)
    vfirst_ref = next(it) if has_vres else None
    g_ref, w_in_ref, mu_ref = next(it), next(it), next(it)
    dup_ref, dbias_ref, aup_ref, abias_ref, gup_ref = (next(it) for _ in range(5))
    kk_ref, ka_ref, bd_ref = next(it), next(it), next(it)
    if has_vres:
        vmu_ref, vdown_ref, vup_ref, vbias_ref = (next(it) for _ in range(4))
    r_ref, lw_ref, k_ref, v_ref, a_ref, b_ref, gate_ref, qmem_ref = (next(it) for _ in range(8))
    carry_p = next(it)
    carry_h = next(it) if has_vres else None

    @pl.when(pl.program_id(1) == 0)
    def _():
        carry_p[...] = jnp.zeros_like(carry_p)
        if has_vres:
            carry_h[...] = jnp.zeros_like(carry_h)

    h = _rms(x_ref[0], g_ref[...])
    proj = _dot(h, w_in_ref[...])
    qmem_ref[0] = proj[:, rwkv_in:]
    p = proj[:, :rwkv_in]
    pm = p + (_shift_rows(p, carry_p) - p) * mu_ref[...]

    r = pm[:, 0:rdim]
    k = pm[:, rdim:2 * rdim]
    v = pm[:, 2 * rdim:3 * rdim]
    lo = pm[:, 3 * rdim:3 * rdim + 128]
    g_lo = pm[:, 3 * rdim + 128:]

    z = dbias_ref[...] + _dot(jnp.tanh(lo), dup_ref[...])
    nz = -z
    softplus = jnp.maximum(nz, 0.0) + jnp.log1p(jnp.exp(-jnp.abs(nz)))
    log_w = -softplus - 0.5
    lw_ref[0] = -jnp.exp(log_w)
    lr = _sigmoid(abias_ref[...] + _dot(lo, aup_ref[...]))
    gate_ref[0] = _dot(_sigmoid(g_lo), gup_ref[...])

    if has_vres:
        hv = h + (_shift_rows(h, carry_h) - h) * vmu_ref[...]
        t = _dot(hv, vdown_ref[...])
        sg = _sigmoid(vbias_ref[...] + _dot(t, vup_ref[...]))
        v = v + (vfirst_ref[0] - v) * sg

    kkr = k * kk_ref[...]
    norm = jnp.sqrt(_dot_ones_rhs(kkr * kkr, bd_ref[...]))
    kk = kkr / jnp.maximum(norm, 1e-12)
    r_ref[0] = r
    k_ref[0] = k * (1.0 + (lr - 1.0) * ka_ref[...])
    v_ref[0] = v
    a_ref[0] = -kk
    b_ref[0] = kk * lr


def _a_pre(x, vfirst, w, tm):
    bsz, s_len, d = x.shape
    rdim = w["k_k"].shape[1]
    n_in = w["w_in"].shape[1]
    rwkv_in = w["mu"].shape[1]
    mem_dim = n_in - rwkv_in
    has_vres = vfirst is not None
    tok = lambda n: pl.BlockSpec((1, tm, n), lambda b, j: (b, j, 0))
    args, specs = [x], [tok(d)]
    if has_vres:
        args.append(vfirst)
        specs.append(tok(rdim))
    names = ["norm_g", "w_in", "mu", "decay_up", "decay_bias", "aaa_up", "aaa_bias", "gate_up",
             "k_k", "k_a", "bd768"]
    if has_vres:
        names += ["vres_mu", "vres_down", "vres_up", "vres_bias"]
    for n in names:
        args.append(w[n])
        specs.append(_const_spec(w[n].shape))
    scratch = [pltpu.VMEM((8, rwkv_in), F32)]
    if has_vres:
        scratch.append(pltpu.VMEM((8, d), F32))
    out_shape = [jax.ShapeDtypeStruct((bsz, s_len, rdim), F32)] * 7 + \
                [jax.ShapeDtypeStruct((bsz, s_len, mem_dim), F32)]
    out_specs = [tok(rdim)] * 7 + [tok(mem_dim)]
    return pl.pallas_call(
        functools.partial(_a_pre_kernel, has_vres, rdim, rwkv_in),
        grid=(bsz, s_len // tm),
        in_specs=specs, out_specs=out_specs, out_shape=out_shape,
        scratch_shapes=scratch,
        compiler_params=_params(("parallel", "arbitrary")),
        name="a_pre_vres" if has_vres else "a_pre",
    )(*args)


def _stack_heads(x_bf16, hm_ref):
    return jnp.concatenate([x_bf16 * hm_ref[h] for h in range(GROUP)], axis=0)


def _scan_chunk(r, lw, k, v, a, b, gate, s_prev, consts, lnx):
    tri, hm_ref, mask2, icat, bd = consts
    lnx_g, lnx_b, r_k = lnx
    c = CHUNK
    gc = GROUP * c
    bd_b = bd[...]
    bd_f = bd_b.astype(F32)

    cl = _dot_ones_lhs(tri[...], lw)
    w_end = jnp.exp(cl[c - 1:c, :])
    e_inc = jnp.exp(cl)
    e_exc = jnp.exp(cl - lw)
    e_inv = jnp.exp(-cl)
    rt = r * e_inc
    at = a * e_exc
    kt = k * e_inv
    bt = b * e_inv

    ar = jnp.concatenate([at, rt], axis=0).astype(BF16)
    bks = jnp.concatenate([_stack_heads(bt.astype(BF16), hm_ref),
                           _stack_heads(kt.astype(BF16), hm_ref)], axis=0)
    m = _dot_nt(ar, bks) * mask2[...]
    l_ab = m[:c, :gc]
    l_ak = m[:c, gc:]
    m_r = m[c:, :]
    ars = _dot_nt(ar, s_prev)

    def blockdiag(p_bf16):
        return jnp.concatenate([p_bf16] * GROUP, axis=0) * bd_b

    pw = l_ab.astype(BF16)
    pw = jnp.dot(pw, blockdiag(pw), preferred_element_type=F32)
    inv = icat[...] + l_ab
    steps = int(np.log2(c)) - 1
    for n in range(steps):
        pbd = blockdiag(pw.astype(BF16))
        if n + 1 < steps:
            both = jnp.dot(jnp.concatenate([inv, pw], axis=0).astype(BF16), pbd,
                           preferred_element_type=F32)
            inv = inv + both[:c]
            pw = both[c:]
        else:
            inv = inv + jnp.dot(inv.astype(BF16), pbd, preferred_element_type=F32)

    vb = v.astype(BF16)
    vs = _stack_heads(vb, hm_ref)
    z = ars[:c] + jnp.dot(l_ak.astype(BF16), vs, preferred_element_type=F32)
    u = jnp.dot(inv.astype(BF16), _stack_heads(z.astype(BF16), hm_ref),
                preferred_element_type=F32)
    ub = u.astype(BF16)
    y = ars[c:] + jnp.dot(m_r.astype(BF16),
                          jnp.concatenate([_stack_heads(ub, hm_ref), vs], axis=0),
                          preferred_element_type=F32)

    uv = jnp.concatenate([ub, vb], axis=0)
    bkh = jnp.concatenate([bt * w_end, kt * w_end], axis=0)
    s_new = s_prev * w_end + bd_f * _dot_tn(uv, bkh)

    mean = _dot_ones_rhs(y, bd_b) * (1.0 / HEAD)
    d = y - mean
    var = _dot_ones_rhs(d * d, bd_b) * (1.0 / HEAD)
    gn = d * lax.rsqrt(var + LNX_EPS) * lnx_g + lnx_b
    bonus = _dot_ones_rhs(r * k * r_k, bd_b) * v
    return (gn + bonus) * gate, s_new


def _scan_kernel(n_groups, n_chunks, r_ref, lw_ref, k_ref, v_ref, a_ref, b_ref, gate_ref,
                 lnxg_ref, lnxb_ref, rk_ref, tri_ref, hm_ref, mask2_ref, icat_ref, bd_ref,
                 o_ref, state_ref):
    @pl.when(pl.program_id(1) == 0)
    def _():
        state_ref[...] = jnp.zeros_like(state_ref)

    consts = (tri_ref, hm_ref, mask2_ref, icat_ref, bd_ref)

    def body(ci, carry):
        rows = pl.ds(pl.multiple_of(ci * CHUNK, CHUNK), CHUNK)
        for g in range(n_groups):
            lanes = slice(g * GL, (g + 1) * GL)
            ld = lambda ref: ref[0, rows, lanes]
            lnx = (lnxg_ref[:, lanes], lnxb_ref[:, lanes], rk_ref[:, lanes])
            out, s_new = _scan_chunk(ld(r_ref), ld(lw_ref), ld(k_ref), ld(v_ref), ld(a_ref),
                                     ld(b_ref), ld(gate_ref), state_ref[g], consts, lnx)
            state_ref[g] = s_new
            o_ref[0, rows, lanes] = out.astype(o_ref.dtype)
        return carry

    lax.fori_loop(0, n_chunks, body, 0)


def _scan_consts():
    c, g = CHUNK, GROUP
    gc = g * c
    i = np.arange(c)[:, None]
    j = np.arange(gc)[None, :] % c
    strict = (j < i).astype(np.float32)
    incl = (j <= i).astype(np.float32)
    mask2 = np.concatenate([np.concatenate([strict, strict], 1),
                            np.concatenate([incl, incl], 1)], 0)
    icat = (j == i).astype(np.float32)
    tri = (np.arange(c)[None, :] <= np.arange(c)[:, None]).astype(np.float32)
    lane_head = np.arange(GL) // HEAD
    hm = (lane_head[None, None, :] == np.arange(g)[:, None, None]).astype(np.float32)
    bd = (lane_head[:, None] == lane_head[None, :]).astype(np.float32)
    return (jnp.asarray(tri, BF16), jnp.asarray(hm, BF16), jnp.asarray(mask2, F32),
            jnp.asarray(icat, F32), jnp.asarray(bd, BF16))


def _scan(r, lw, k, v, a, b, gate, lnx_g, lnx_b, r_k, tc):
    bsz, s_len, rdim = r.shape
    n_groups = rdim // GL
    tok = pl.BlockSpec((1, tc, rdim), lambda bi, j: (bi, j, 0))
    consts = _scan_consts()
    small = [lnx_g, lnx_b, r_k] + list(consts)
    return pl.pallas_call(
        functools.partial(_scan_kernel, n_groups, tc // CHUNK),
        grid=(bsz, s_len // tc),
        in_specs=[tok] * 7 + [_const_spec(t.shape) for t in small],
        out_specs=tok,
        out_shape=jax.ShapeDtypeStruct((bsz, s_len, rdim), BF16),
        scratch_shapes=[pltpu.VMEM((n_groups, GL, GL), F32)],
        compiler_params=_params(("parallel", "arbitrary")),
        name="rwkv_scan",
    )(r, lw, k, v, a, b, gate, *small)


def _post_kernel(mem_heads, ffn_tiles, final, x_ref, mix_ref, qmem_ref, mk_ref, mv_ref, hm_ref,
                 w_out_ref, g2_ref, w_gu_ref, w_dn_ref, gf_ref, o_ref):
    mix_dim = mix_ref.shape[2]
    fh = w_dn_ref.shape[0]
    th = fh // ffn_tiles

    qm = (qmem_ref[0] * (HEAD ** -0.5)).astype(BF16)
    mk = mk_ref[0]
    mv = mv_ref[0]
    m = None
    for hd in range(mem_heads):
        s = _dot_nt(qm * hm_ref[hd], mk)
        s = s - jnp.max(s, axis=-1, keepdims=True)
        e = jnp.exp(s)
        p = e / jnp.sum(e, axis=-1, keepdims=True)
        o = _dot(p, mv) * hm_ref[hd].astype(F32)
        m = o if m is None else m + o

    x1 = x_ref[0] + _dot(mix_ref[0], w_out_ref[0:mix_dim, :]) + _dot(m, w_out_ref[mix_dim:, :])
    h2 = _rms(x1, g2_ref[...]).astype(BF16)
    acc = None
    for t in range(ffn_tiles):
        gt = jnp.dot(h2, w_gu_ref[:, t * th:(t + 1) * th], preferred_element_type=F32)
        ut = jnp.dot(h2, w_gu_ref[:, fh + t * th:fh + (t + 1) * th], preferred_element_type=F32)
        act = gt * _sigmoid(gt) * ut
        dt = _dot(act, w_dn_ref[t * th:(t + 1) * th, :])
        acc = dt if acc is None else acc + dt
    x2 = x1 + acc
    if final:
        x2 = _rms(x2, gf_ref[...])
    o_ref[0] = x2


def _post(x, mix, qmem, memkv, layer, w, final_g, tm):
    bsz, s_len, d = x.shape
    mix_dim = mix.shape[2]
    mem_dim = qmem.shape[2]
    n_mem = memkv.shape[1]
    mem_heads = mem_dim // HEAD
    fh = w["ffn_down"].shape[0]
    ffn_tiles = 2 if fh % 256 == 0 else 1
    lane_head = np.arange(mem_dim) // HEAD
    hm = jnp.asarray((lane_head[None, None, :] == np.arange(mem_heads)[:, None, None]), BF16)
    tok = lambda n: pl.BlockSpec((1, tm, n), lambda b, j: (b, j, 0))
    final = final_g is not None
    gf = final_g if final else w["norm2_g"]
    return pl.pallas_call(
        functools.partial(_post_kernel, mem_heads, ffn_tiles, final),
        grid=(bsz, s_len // tm),
        in_specs=[tok(d), tok(mix_dim), tok(mem_dim),
                  pl.BlockSpec((1, n_mem, mem_dim), lambda b, j: (b, 0, 2 * layer)),
                  pl.BlockSpec((1, n_mem, mem_dim), lambda b, j: (b, 0, 2 * layer + 1)),
                  _const_spec(hm.shape), _const_spec(w["w_out"].shape),
                  _const_spec(w["norm2_g"].shape), _const_spec(w["ffn_gu"].shape),
                  _const_spec(w["ffn_down"].shape), _const_spec(gf.shape)],
        out_specs=tok(d),
        out_shape=jax.ShapeDtypeStruct((bsz, s_len, d), F32),
        compiler_params=_params(("parallel", "parallel")),
        name="post_final" if final else "post",
    )(x, mix, qmem, memkv, memkv, hm, w["w_out"], w["norm2_g"], w["ffn_gu"], w["ffn_down"], gf)


def _kv_kernel(lat, x_ref, cos_ref, sin_ref, g_ref, wd_ref, lg_ref, wk_ref, wv_ref, k_ref, v_ref):
    hk = _rms(x_ref[0], g_ref[...])
    ckr = _dot(hk, wd_ref[...])
    ckv = _rms(ckr[:, :lat], lg_ref[...])
    krot = ckr[:, lat:lat + MLA_PAD] * cos_ref[0] + ckr[:, lat + MLA_PAD:] * sin_ref[0]
    kn = _dot(ckv, wk_ref[...])
    heads = kn.shape[1] // MLA_PAD
    k_ref[0] = (kn + jnp.concatenate([krot] * heads, axis=1)).astype(k_ref.dtype)
    v_ref[0] = _dot(ckv, wv_ref[...]).astype(v_ref.dtype)


def _kv(x, cos_t, sin_t, w, tm):
    bsz, s_len, d = x.shape
    lat = w["kv_latent_g"].shape[1]
    nk = w["kv_wk"].shape[1]
    nv = w["kv_wv"].shape[1]
    tok = lambda n: pl.BlockSpec((1, tm, n), lambda b, j: (b, j, 0))
    names = ["kv_norm_g", "kv_wd", "kv_latent_g", "kv_wk", "kv_wv"]
    return pl.pallas_call(
        functools.partial(_kv_kernel, lat),
        grid=(bsz, s_len // tm),
        in_specs=[tok(d), tok(MLA_PAD), tok(MLA_PAD)] + [_const_spec(w[n].shape) for n in names],
        out_specs=[tok(nk), tok(nv)],
        out_shape=[jax.ShapeDtypeStruct((bsz, s_len, nk), BF16),
                   jax.ShapeDtypeStruct((bsz, s_len, nv), BF16)],
        compiler_params=_params(("parallel", "parallel")),
        name="mla_kv",
    )(x, cos_t, sin_t, *[w[n] for n in names])


def _b_pre_kernel(q_lora, scale, x_ref, cos_ref, sin_ref, g_ref, w_in_ref, qg_ref, qa_ref, qb_ref,
                  q_ref, qmem_ref):
    h = _rms(x_ref[0], g_ref[...])
    proj = _dot(h, w_in_ref[...])
    qmem_ref[0] = proj[:, q_lora:]
    cq = _rms(proj[:, :q_lora], qg_ref[...]).astype(BF16)
    qa = jnp.dot(cq, qa_ref[...], preferred_element_type=F32)
    qb = jnp.dot(cq, qb_ref[...], preferred_element_type=F32)
    heads = qa.shape[1] // MLA_PAD
    cos_t = jnp.concatenate([cos_ref[0]] * heads, axis=1)
    sin_t = jnp.concatenate([sin_ref[0]] * heads, axis=1)
    q_ref[0] = ((qa * cos_t + qb * sin_t) * scale).astype(q_ref.dtype)


def _b_pre(x, cos_t, sin_t, w, tm):
    bsz, s_len, d = x.shape
    q_lora = w["q_norm_g"].shape[1]
    mem_dim = w["w_in"].shape[1] - q_lora
    nq = w["q_up_a"].shape[1]
    scale = (QK_NOPE + QK_ROPE) ** -0.5
    tok = lambda n: pl.BlockSpec((1, tm, n), lambda b, j: (b, j, 0))
    names = ["norm_g", "w_in", "q_norm_g", "q_up_a", "q_up_b"]
    return pl.pallas_call(
        functools.partial(_b_pre_kernel, q_lora, scale),
        grid=(bsz, s_len // tm),
        in_specs=[tok(d), tok(MLA_PAD), tok(MLA_PAD)] + [_const_spec(w[n].shape) for n in names],
        out_specs=[tok(nq), tok(mem_dim)],
        out_shape=[jax.ShapeDtypeStruct((bsz, s_len, nq), BF16),
                   jax.ShapeDtypeStruct((bsz, s_len, mem_dim), F32)],
        compiler_params=_params(("parallel", "parallel")),
        name="mla_q",
    )(x, cos_t, sin_t, *[w[n] for n in names])


def _flash_kernel(tq, tk, q_ref, k_ref, v_ref, o_ref, m_ref, l_ref, acc_ref):
    i = pl.program_id(2)
    j = pl.program_id(3)
    last_j = ((i + 1) * tq - 1) // tk

    @pl.when(j == 0)
    def _():
        m_ref[...] = jnp.full_like(m_ref, -jnp.inf)
        l_ref[...] = jnp.zeros_like(l_ref)
        acc_ref[...] = jnp.zeros_like(acc_ref)

    @pl.when(j <= last_j)
    def _():
        lane = lax.broadcasted_iota(jnp.int32, (1, 2 * HEAD), 1)
        low = lane < HEAD
        vpair = v_ref[0]
        rel = (lax.broadcasted_iota(jnp.int32, (tq, tk), 0)
               - lax.broadcasted_iota(jnp.int32, (tq, tk), 1))
        visible = rel >= j * tk - i * tq
        alphas, pvs = [], []
        for hd in range(2):
            q = q_ref[0][:, hd * MLA_PAD:(hd + 1) * MLA_PAD]
            k = k_ref[0][:, hd * MLA_PAD:(hd + 1) * MLA_PAD]
            s = jnp.where(visible, _dot_nt(q, k), -jnp.inf)
            m_prev = m_ref[hd]
            m_new = jnp.maximum(m_prev, jnp.max(s, axis=-1, keepdims=True))
            alpha = jnp.exp(m_prev - m_new)
            p = jnp.exp(s - m_new)
            l_ref[hd] = alpha * l_ref[hd] + jnp.sum(p, axis=-1, keepdims=True)
            m_ref[hd] = m_new
            vh = jnp.where(low, vpair, 0) if hd == 0 else jnp.where(low, 0, vpair)
            pvs.append(jnp.dot(p.astype(BF16), vh.astype(BF16), preferred_element_type=F32))
            alphas.append(alpha)
        acc_ref[...] = acc_ref[...] * jnp.where(low, alphas[0], alphas[1]) + pvs[0] + pvs[1]

    @pl.when(j == last_j)
    def _():
        lane = lax.broadcasted_iota(jnp.int32, (1, 2 * HEAD), 1)
        l_pair = jnp.where(lane < HEAD, l_ref[0], l_ref[1])
        o_ref[0] = (acc_ref[...] / l_pair).astype(o_ref.dtype)


def _flash(q, k, v, tq, tk):
    bsz, s_len, nq = q.shape
    pairs = nq // (2 * MLA_PAD)
    last = lambda i: ((i + 1) * tq - 1) // tk
    return pl.pallas_call(
        functools.partial(_flash_kernel, tq, tk),
        grid=(bsz, pairs, s_len // tq, s_len // tk),
        in_specs=[pl.BlockSpec((1, tq, 2 * MLA_PAD), lambda b, p, i, j: (b, i, p)),
                  pl.BlockSpec((1, tk, 2 * MLA_PAD), lambda b, p, i, j: (b, jnp.minimum(j, last(i)), p)),
                  pl.BlockSpec((1, tk, 2 * HEAD), lambda b, p, i, j: (b, jnp.minimum(j, last(i)), p))],
        out_specs=pl.BlockSpec((1, tq, 2 * HEAD), lambda b, p, i, j: (b, i, p)),
        out_shape=jax.ShapeDtypeStruct((bsz, s_len, pairs * 2 * HEAD), BF16),
        scratch_shapes=[pltpu.VMEM((2, tq, 1), F32), pltpu.VMEM((2, tq, 1), F32),
                        pltpu.VMEM((tq, 2 * HEAD), F32)],
        compiler_params=_params(("parallel", "parallel", "parallel", "arbitrary")),
        name="mla_flash",
    )(q, k, v)


def _row(v):
    return v.reshape(1, -1).astype(F32)


def _pad_rows(w, total, offset):
    out = jnp.zeros((total, w.shape[1]), w.dtype)
    return out.at[offset:offset + w.shape[0]].set(w)


def _block_ones(n):
    h = np.arange(n) // HEAD
    return jnp.asarray(h[:, None] == h[None, :], BF16)


def _rope_cols(w_rope, swap):
    half = QK_ROPE // 2
    t1, t2 = w_rope[:, :half], w_rope[:, half:]
    if swap:
        t1, t2 = t2, t1
    z = lambda n: jnp.zeros((w_rope.shape[0], n), w_rope.dtype)
    return jnp.concatenate([z(QK_NOPE), t1, t2, z(MLA_PAD - QK_NOPE - QK_ROPE)], axis=1)


def _q_up_padded(q_up, heads, swap):
    per = QK_NOPE + QK_ROPE
    blocks = []
    for hd in range(heads):
        wh = q_up[:, hd * per:(hd + 1) * per]
        rope = _rope_cols(wh[:, QK_NOPE:], swap)
        if swap:
            blocks.append(rope)
        else:
            blocks.append(rope.at[:, :QK_NOPE].set(wh[:, :QK_NOPE]))
    return jnp.concatenate(blocks, axis=1).astype(BF16)


def kernel(x, mem, positions, mem_norm_g, a_norm1_g, a_w_in, a_shift_mu, a_decay_up, a_decay_bias, a_aaa_up, a_aaa_bias, a_gate_up, a_k_k, a_k_a, a_r_k, a_lnx_g, a_lnx_b, a_mem_kv, a_w_out, a_norm2_g, a_ffn_gu, a_ffn_down, vres_mu, vres_down, vres_up, vres_bias, kv_norm_g, kv_w_down, kv_latent_g, kv_w_up, b_norm1_g, b_w_in, b_q_norm_g, b_q_up, b_mem_kv, b_w_out, b_norm2_g, b_ffn_gu, b_ffn_down, final_norm_g):
    bsz, s_len, d = x.shape
    n_a = a_w_in.shape[0]
    n_b = b_w_in.shape[0]
    rdim = a_k_k.shape[1]
    assert CHUNK == HEAD and rdim % GL == 0 and s_len % CHUNK == 0
    tm = _tile(s_len, TILES["tm"])
    tc = _tile(s_len, TILES["tc"])
    tq = _tile(s_len, TILES["tq"])
    tk = _tile(s_len, TILES["tk"])

    half = QK_ROPE // 2
    inv_freq = 10000.0 ** (-jnp.arange(half, dtype=F32) / half)
    ang = positions.astype(F32)[..., None] * inv_freq
    cos, sin = jnp.cos(ang), jnp.sin(ang)
    ones = jnp.ones((bsz, s_len, QK_NOPE), F32)
    zpad = jnp.zeros((bsz, s_len, MLA_PAD - QK_NOPE - QK_ROPE), F32)
    cos_t = jnp.concatenate([ones, cos, cos, zpad], axis=-1)
    sin_t = jnp.concatenate([0 * ones, -sin, sin, zpad], axis=-1)

    memkv = _memkv(mem, _row(mem_norm_g),
                   jnp.concatenate(list(a_mem_kv) + list(b_mem_kv), axis=1).astype(BF16))
    bd768 = _block_ones(rdim)

    v_first = None
    for i in range(n_a):
        w = dict(norm_g=_row(a_norm1_g[i]), w_in=a_w_in[i].astype(BF16), mu=_row(a_shift_mu[i]),
                 decay_up=_pad_rows(a_decay_up[i], 128, 0).astype(BF16),
                 decay_bias=_row(a_decay_bias[i]),
                 aaa_up=_pad_rows(a_aaa_up[i], 128, a_decay_up.shape[1]).astype(BF16),
                 aaa_bias=_row(a_aaa_bias[i]), gate_up=a_gate_up[i].astype(BF16),
                 k_k=_row(a_k_k[i]), k_a=_row(a_k_a[i]), bd768=bd768)
        if i > 0:
            w.update(vres_mu=_row(vres_mu[i - 1]),
                     vres_down=jnp.pad(vres_down[i - 1], ((0, 0), (0, 128 - vres_down.shape[2]))).astype(BF16),
                     vres_up=_pad_rows(vres_up[i - 1], 128, 0).astype(BF16),
                     vres_bias=_row(vres_bias[i - 1]))
        r, lw, k, v, a, b, gate, qmem = _a_pre(x, v_first if i > 0 else None, w, tm)
        if i == 0:
            v_first = v
        y_mix = _scan(r, lw, k, v, a, b, gate, _row(a_lnx_g[i]), _row(a_lnx_b[i]),
                      _row(a_r_k[i]), tc)
        wp = dict(w_out=a_w_out[i].astype(BF16), norm2_g=_row(a_norm2_g[i]),
                  ffn_gu=a_ffn_gu[i].astype(BF16), ffn_down=a_ffn_down[i].astype(BF16))
        x = _post(x, y_mix, qmem, memkv, i, wp, None, tm)

    lat = kv_latent_g.shape[0]
    heads = b_q_up.shape[2] // (QK_NOPE + QK_ROPE)
    per_kv = kv_w_up.shape[1] // heads
    wk_blocks, wv_blocks = [], []
    for hd in range(heads):
        blk = kv_w_up[:, hd * per_kv:(hd + 1) * per_kv]
        wk_blocks.append(jnp.pad(blk[:, :QK_NOPE], ((0, 0), (0, MLA_PAD - QK_NOPE))))
        wv_blocks.append(blk[:, QK_NOPE:])
    wkv = dict(kv_norm_g=_row(kv_norm_g),
               kv_wd=jnp.concatenate([kv_w_down[:, :lat], _rope_cols(kv_w_down[:, lat:], False),
                                      _rope_cols(kv_w_down[:, lat:], True)], axis=1).astype(BF16),
               kv_latent_g=_row(kv_latent_g),
               kv_wk=jnp.concatenate(wk_blocks, axis=1).astype(BF16),
               kv_wv=jnp.concatenate(wv_blocks, axis=1).astype(BF16))
    k_all, v_all = _kv(x, cos_t, sin_t, wkv, tm)

    for j in range(n_b):
        wq = dict(norm_g=_row(b_norm1_g[j]), w_in=b_w_in[j].astype(BF16),
                  q_norm_g=_row(b_q_norm_g[j]),
                  q_up_a=_q_up_padded(b_q_up[j], heads, False),
                  q_up_b=_q_up_padded(b_q_up[j], heads, True))
        q, qmem = _b_pre(x, cos_t, sin_t, wq, tm)
        o = _flash(q, k_all, v_all, tq, tk)
        wp = dict(w_out=b_w_out[j].astype(BF16), norm2_g=_row(b_norm2_g[j]),
                  ffn_gu=b_ffn_gu[j].astype(BF16), ffn_down=b_ffn_down[j].astype(BF16))
        x = _post(x, o, qmem, memkv, n_a + j, wp,
                  _row(final_norm_g) if j == n_b - 1 else None, tm)
    return x
```

```python
import functools

import numpy as np
import jax
import jax.numpy as jnp
from jax import lax
from jax.experimental import pallas as pl
from jax.experimental.pallas import tpu as pltpu

F32 = jnp.float32
BF16 = jnp.bfloat16

HEAD = 64
CHUNK = 64
GROUP = 4
GL = GROUP * HEAD
NORM_EPS = 1e-6
LNX_EPS = 64e-5
QK_NOPE = 64
QK_ROPE = 32
MLA_PAD = 128
VMEM_LIMIT = 56 * 1024 * 1024
TILES = dict(tm=512, tc=512, tq=512)


def _dot(a, b):
    return jnp.dot(a.astype(BF16), b.astype(BF16), preferred_element_type=F32)


def _dot_nt(a, b):
    return lax.dot_general(a.astype(BF16), b.astype(BF16), (((1,), (1,)), ((), ())),
                           preferred_element_type=F32)


def _dot_tn(a, b):
    return lax.dot_general(a.astype(BF16), b.astype(BF16), (((0,), (0,)), ((), ())),
                           preferred_element_type=F32)


def _split(x, terms):
    out = []
    for _ in range(terms):
        p = x.astype(BF16)
        out.append(p)
        x = x - p.astype(F32)
    return out


def _dot_ones_rhs(x, ones_bf16, terms=2):
    acc = None
    for p in _split(x, terms):
        d = jnp.dot(p, ones_bf16, preferred_element_type=F32)
        acc = d if acc is None else acc + d
    return acc


def _dot_ones_lhs(ones_bf16, x, terms=3):
    acc = None
    for p in _split(x, terms):
        d = jnp.dot(ones_bf16, p, preferred_element_type=F32)
        acc = d if acc is None else acc + d
    return acc


def _rms(x, g, eps=NORM_EPS):
    ms = jnp.mean(x * x, axis=-1, keepdims=True)
    return x * lax.rsqrt(ms + eps) * g


def _sigmoid(x):
    return 1.0 / (1.0 + jnp.exp(-x))


def _shift_rows(x, carry_ref):
    tm = x.shape[0]
    first = lax.broadcasted_iota(jnp.int32, (tm, 1), 0) == 0
    prev = jnp.where(first, carry_ref[0:1, :], pltpu.roll(x, 1, 0))
    carry_ref[0:1, :] = x[tm - 1:tm, :]
    return prev


def _const_spec(shape):
    nd = len(shape)
    return pl.BlockSpec(shape, lambda *_: (0,) * nd, pipeline_mode=pl.Buffered(1))


def _params(sem):
    return pltpu.CompilerParams(dimension_semantics=sem, vmem_limit_bytes=VMEM_LIMIT)


def _tile(n, pref):
    t = min(n, pref)
    assert n % t == 0, (n, t)
    return t


def _memkv_kernel(mem_ref, g_ref, w_ref, o_ref):
    mn = _rms(mem_ref[0], g_ref[...])
    o_ref[0] = _dot(mn, w_ref[...]).astype(o_ref.dtype)


def _memkv(mem, g, w_all):
    bsz, n_mem, d = mem.shape
    n = w_all.shape[1]
    return pl.pallas_call(
        _memkv_kernel,
        grid=(bsz,),
        in_specs=[pl.BlockSpec((1, n_mem, d), lambda b: (b, 0, 0)),
                  _const_spec((1, d)), _const_spec((d, n))],
        out_specs=pl.BlockSpec((1, n_mem, n), lambda b: (b, 0, 0)),
        out_shape=jax.ShapeDtypeStruct((bsz, n_mem, n), BF16),
        compiler_params=_params(("parallel",)),
        name="memkv",
    )(mem, g, w_all)


def _a_pre_kernel(has_vres, rdim, rwkv_in, *refs):
    it = iter(refs)
    x_ref = next(it)
    vfirst_ref = next(it) if has_vres else None
    g_ref, w_in_ref, mu_ref = next(it), next(it), next(it)
    dup_ref, dbias_ref, aup_ref, abias_ref, gup_ref = (next(it) for _ in range(5))
    kk_ref, ka_ref, bd_ref = next(it), next(it), next(it)
    if has_vres:
        vmu_ref, vdown_ref, vup_ref, vbias_ref = (next(it) for _ in range(4))
    r_ref, lw_ref, k_ref, v_ref, a_ref, b_ref, gate_ref, qmem_ref = (next(it) for _ in range(8))
    carry_p = next(it)
    carry_h = next(it) if has_vres else None

    @pl.when(pl.program_id(1) == 0)
    def _():
        carry_p[...] = jnp.zeros_like(carry_p)
        if has_vres:
            carry_h[...] = jnp.zeros_like(carry_h)

    h = _rms(x_ref[0], g_ref[...])
    proj = _dot(h, w_in_ref[...])
    qmem_ref[0] = proj[:, rwkv_in:]
    p = proj[:, :rwkv_in]
    pm = p + (_shift_rows(p, carry_p) - p) * mu_ref[...]

    r = pm[:, 0:rdim]
    k = pm[:, rdim:2 * rdim]
    v = pm[:, 2 * rdim:3 * rdim]
    lo = pm[:, 3 * rdim:3 * rdim + 128]
    g_lo = pm[:, 3 * rdim + 128:]

    z = dbias_ref[...] + _dot(jnp.tanh(lo), dup_ref[...])
    nz = -z
    softplus = jnp.maximum(nz, 0.0) + jnp.log1p(jnp.exp(-jnp.abs(nz)))
    log_w = -softplus - 0.5
    lw_ref[0] = -jnp.exp(log_w)
    lr = _sigmoid(abias_ref[...] + _dot(lo, aup_ref[...]))
    gate_ref[0] = _dot(_sigmoid(g_lo), gup_ref[...])

    if has_vres:
        hv = h + (_shift_rows(h, carry_h) - h) * vmu_ref[...]
        t = _dot(hv, vdown_ref[...])
        sg = _sigmoid(vbias_ref[...] + _dot(t, vup_ref[...]))
        v = v + (vfirst_ref[0] - v) * sg

    kkr = k * kk_ref[...]
    norm = jnp.sqrt(_dot_ones_rhs(kkr * kkr, bd_ref[...]))
    kk = kkr / jnp.maximum(norm, 1e-12)
    r_ref[0] = r
    k_ref[0] = k * (1.0 + (lr - 1.0) * ka_ref[...])
    v_ref[0] = v
    a_ref[0] = -kk
    b_ref[0] = kk * lr


def _a_pre(x, vfirst, w, tm):
    bsz, s_len, d = x.shape
    rdim = w["k_k"].shape[1]
    n_in = w["w_in"].shape[1]
    rwkv_in = w["mu"].shape[1]
    mem_dim = n_in - rwkv_in
    has_vres = vfirst is not None
    tok = lambda n: pl.BlockSpec((1, tm, n), lambda b, j: (b, j, 0))
    args, specs = [x], [tok(d)]
    if has_vres:
        args.append(vfirst)
        specs.append(tok(rdim))
    names = ["norm_g", "w_in", "mu", "decay_up", "decay_bias", "aaa_up", "aaa_bias", "gate_up",
             "k_k", "k_a", "bd768"]
    if has_vres:
        names += ["vres_mu", "vres_down", "vres_up", "vres_bias"]
    for n in names:
        args.append(w[n])
        specs.append(_const_spec(w[n].shape))
    scratch = [pltpu.VMEM((8, rwkv_in), F32)]
    if has_vres:
        scratch.append(pltpu.VMEM((8, d), F32))
    out_shape = [jax.ShapeDtypeStruct((bsz, s_len, rdim), F32)] * 7 + \
                [jax.ShapeDtypeStruct((bsz, s_len, mem_dim), F32)]
    out_specs = [tok(rdim)] * 7 + [tok(mem_dim)]
    return pl.pallas_call(
        functools.partial(_a_pre_kernel, has_vres, rdim, rwkv_in),
        grid=(bsz, s_len // tm),
        in_specs=specs, out_specs=out_specs, out_shape=out_shape,
        scratch_shapes=scratch,
        compiler_params=_params(("parallel", "arbitrary")),
        name="a_pre_vres" if has_vres else "a_pre",
    )(*args)


def _stack_heads(x_bf16, hm_ref):
    return jnp.concatenate([x_bf16 * hm_ref[h] for h in range(GROUP)], axis=0)


def _scan_chunk(r, lw, k, v, a, b, gate, s_prev, consts, lnx):
    tri, hm_ref, mask2, icat, bd = consts
    lnx_g, lnx_b, r_k = lnx
    c = CHUNK
    gc = GROUP * c
    bd_b = bd[...]
    bd_f = bd_b.astype(F32)

    cl = _dot_ones_lhs(tri[...], lw)
    w_end = jnp.exp(cl[c - 1:c, :])
    e_inc = jnp.exp(cl)
    e_exc = jnp.exp(cl - lw)
    e_inv = jnp.exp(-cl)
    rt = r * e_inc
    at = a * e_exc
    kt = k * e_inv
    bt = b * e_inv

    ar = jnp.concatenate([at, rt], axis=0).astype(BF16)
    bks = jnp.concatenate([_stack_heads(bt.astype(BF16), hm_ref),
                           _stack_heads(kt.astype(BF16), hm_ref)], axis=0)
    m = _dot_nt(ar, bks) * mask2[...]
    l_ab = m[:c, :gc]
    l_ak = m[:c, gc:]
    m_r = m[c:, :]
    ars = _dot_nt(ar, s_prev)

    def blockdiag(p_bf16):
        return jnp.concatenate([p_bf16] * GROUP, axis=0) * bd_b

    pw = l_ab.astype(BF16)
    pw = jnp.dot(pw, blockdiag(pw), preferred_element_type=F32)
    inv = icat[...] + l_ab
    steps = int(np.log2(c)) - 1
    for n in range(steps):
        pbd = blockdiag(pw.astype(BF16))
        if n + 1 < steps:
            both = jnp.dot(jnp.concatenate([inv, pw], axis=0).astype(BF16), pbd,
                           preferred_element_type=F32)
            inv = inv + both[:c]
            pw = both[c:]
        else:
            inv = inv + jnp.dot(inv.astype(BF16), pbd, preferred_element_type=F32)

    vb = v.astype(BF16)
    vs = _stack_heads(vb, hm_ref)
    z = ars[:c] + jnp.dot(l_ak.astype(BF16), vs, preferred_element_type=F32)
    u = jnp.dot(inv.astype(BF16), _stack_heads(z.astype(BF16), hm_ref),
                preferred_element_type=F32)
    ub = u.astype(BF16)
    y = ars[c:] + jnp.dot(m_r.astype(BF16),
                          jnp.concatenate([_stack_heads(ub, hm_ref), vs], axis=0),
                          preferred_element_type=F32)

    uv = jnp.concatenate([ub, vb], axis=0)
    bkh = jnp.concatenate([bt * w_end, kt * w_end], axis=0)
    s_new = s_prev * w_end + bd_f * _dot_tn(uv, bkh)

    mean = _dot_ones_rhs(y, bd_b) * (1.0 / HEAD)
    d = y - mean
    var = _dot_ones_rhs(d * d, bd_b) * (1.0 / HEAD)
    gn = d * lax.rsqrt(var + LNX_EPS) * lnx_g + lnx_b
    bonus = _dot_ones_rhs(r * k * r_k, bd_b) * v
    return (gn + bonus) * gate, s_new


def _scan_kernel(n_groups, n_chunks, r_ref, lw_ref, k_ref, v_ref, a_ref, b_ref, gate_ref,
                 lnxg_ref, lnxb_ref, rk_ref, tri_ref, hm_ref, mask2_ref, icat_ref, bd_ref,
                 o_ref, state_ref):
    @pl.when(pl.program_id(1) == 0)
    def _():
        state_ref[...] = jnp.zeros_like(state_ref)

    consts = (tri_ref, hm_ref, mask2_ref, icat_ref, bd_ref)

    def body(ci, carry):
        rows = pl.ds(pl.multiple_of(ci * CHUNK, CHUNK), CHUNK)
        for g in range(n_groups):
            lanes = slice(g * GL, (g + 1) * GL)
            ld = lambda ref: ref[0, rows, lanes]
            lnx = (lnxg_ref[:, lanes], lnxb_ref[:, lanes], rk_ref[:, lanes])
            out, s_new = _scan_chunk(ld(r_ref), ld(lw_ref), ld(k_ref), ld(v_ref), ld(a_ref),
                                     ld(b_ref), ld(gate_ref), state_ref[g], consts, lnx)
            state_ref[g] = s_new
            o_ref[0, rows, lanes] = out.astype(o_ref.dtype)
        return carry

    lax.fori_loop(0, n_chunks, body, 0)


def _scan_consts():
    c, g = CHUNK, GROUP
    gc = g * c
    i = np.arange(c)[:, None]
    j = np.arange(gc)[None, :] % c
    strict = (j < i).astype(np.float32)
    incl = (j <= i).astype(np.float32)
    mask2 = np.concatenate([np.concatenate([strict, strict], 1),
                            np.concatenate([incl, incl], 1)], 0)
    icat = (j == i).astype(np.float32)
    tri = (np.arange(c)[None, :] <= np.arange(c)[:, None]).astype(np.float32)
    lane_head = np.arange(GL) // HEAD
    hm = (lane_head[None, None, :] == np.arange(g)[:, None, None]).astype(np.float32)
    bd = (lane_head[:, None] == lane_head[None, :]).astype(np.float32)
    return (jnp.asarray(tri, BF16), jnp.asarray(hm, BF16), jnp.asarray(mask2, F32),
            jnp.asarray(icat, F32), jnp.asarray(bd, BF16))


def _scan(r, lw, k, v, a, b, gate, lnx_g, lnx_b, r_k, tc):
    bsz, s_len, rdim = r.shape
    n_groups = rdim // GL
    tok = pl.BlockSpec((1, tc, rdim), lambda bi, j: (bi, j, 0))
    consts = _scan_consts()
    small = [lnx_g, lnx_b, r_k] + list(consts)
    return pl.pallas_call(
        functools.partial(_scan_kernel, n_groups, tc // CHUNK),
        grid=(bsz, s_len // tc),
        in_specs=[tok] * 7 + [_const_spec(t.shape) for t in small],
        out_specs=tok,
        out_shape=jax.ShapeDtypeStruct((bsz, s_len, rdim), BF16),
        scratch_shapes=[pltpu.VMEM((n_groups, GL, GL), F32)],
        compiler_params=_params(("parallel", "arbitrary")),
        name="rwkv_scan",
    )(r, lw, k, v, a, b, gate, *small)


def _post_kernel(mem_heads, ffn_tiles, final, x_ref, mix_ref, qmem_ref, mk_ref, mv_ref, hm_ref,
                 w_out_ref, g2_ref, w_gu_ref, w_dn_ref, gf_ref, o_ref):
    mix_dim = mix_ref.shape[2]
    fh = w_dn_ref.shape[0]
    th = fh // ffn_tiles

    qm = (qmem_ref[0] * (HEAD ** -0.5)).astype(BF16)
    mk = mk_ref[0]
    mv = mv_ref[0]
    m = None
    for hd in range(mem_heads):
        s = _dot_nt(qm * hm_ref[hd], mk)
        s = s - jnp.max(s, axis=-1, keepdims=True)
        e = jnp.exp(s)
        p = e / jnp.sum(e, axis=-1, keepdims=True)
        o = _dot(p, mv) * hm_ref[hd].astype(F32)
        m = o if m is None else m + o

    x1 = x_ref[0] + _dot(mix_ref[0], w_out_ref[0:mix_dim, :]) + _dot(m, w_out_ref[mix_dim:, :])
    h2 = _rms(x1, g2_ref[...]).astype(BF16)
    acc = None
    for t in range(ffn_tiles):
        gt = jnp.dot(h2, w_gu_ref[:, t * th:(t + 1) * th], preferred_element_type=F32)
        ut = jnp.dot(h2, w_gu_ref[:, fh + t * th:fh + (t + 1) * th], preferred_element_type=F32)
        act = gt * _sigmoid(gt) * ut
        dt = _dot(act, w_dn_ref[t * th:(t + 1) * th, :])
        acc = dt if acc is None else acc + dt
    x2 = x1 + acc
    if final:
        x2 = _rms(x2, gf_ref[...])
    o_ref[0] = x2


def _post(x, mix, qmem, memkv, layer, w, final_g, tm):
    bsz, s_len, d = x.shape
    mix_dim = mix.shape[2]
    mem_dim = qmem.shape[2]
    n_mem = memkv.shape[1]
    mem_heads = mem_dim // HEAD
    fh = w["ffn_down"].shape[0]
    ffn_tiles = 2 if fh % 256 == 0 else 1
    lane_head = np.arange(mem_dim) // HEAD
    hm = jnp.asarray((lane_head[None, None, :] == np.arange(mem_heads)[:, None, None]), BF16)
    tok = lambda n: pl.BlockSpec((1, tm, n), lambda b, j: (b, j, 0))
    final = final_g is not None
    gf = final_g if final else w["norm2_g"]
    return pl.pallas_call(
        functools.partial(_post_kernel, mem_heads, ffn_tiles, final),
        grid=(bsz, s_len // tm),
        in_specs=[tok(d), tok(mix_dim), tok(mem_dim),
                  pl.BlockSpec((1, n_mem, mem_dim), lambda b, j: (b, 0, 2 * layer)),
                  pl.BlockSpec((1, n_mem, mem_dim), lambda b, j: (b, 0, 2 * layer + 1)),
                  _const_spec(hm.shape), _const_spec(w["w_out"].shape),
                  _const_spec(w["norm2_g"].shape), _const_spec(w["ffn_gu"].shape),
                  _const_spec(w["ffn_down"].shape), _const_spec(gf.shape)],
        out_specs=tok(d),
        out_shape=jax.ShapeDtypeStruct((bsz, s_len, d), F32),
        compiler_params=_params(("parallel", "parallel")),
        name="post_final" if final else "post",
    )(x, mix, qmem, memkv, memkv, hm, w["w_out"], w["norm2_g"], w["ffn_gu"], w["ffn_down"], gf)


def _kv_kernel(lat, x_ref, cos_ref, sin_ref, g_ref, wd_ref, lg_ref, wk_ref, wv_ref, k_ref, v_ref):
    hk = _rms(x_ref[0], g_ref[...])
    ckr = _dot(hk, wd_ref[...])
    ckv = _rms(ckr[:, :lat], lg_ref[...])
    krot = ckr[:, lat:lat + MLA_PAD] * cos_ref[0] + ckr[:, lat + MLA_PAD:] * sin_ref[0]
    kn = _dot(ckv, wk_ref[...])
    heads = kn.shape[1] // MLA_PAD
    k_ref[0] = (kn + jnp.concatenate([krot] * heads, axis=1)).astype(k_ref.dtype)
    v_ref[0] = _dot(ckv, wv_ref[...]).astype(v_ref.dtype)


def _kv(x, cos_t, sin_t, w, tm):
    bsz, s_len, d = x.shape
    lat = w["kv_latent_g"].shape[1]
    nk = w["kv_wk"].shape[1]
    nv = w["kv_wv"].shape[1]
    tok = lambda n: pl.BlockSpec((1, tm, n), lambda b, j: (b, j, 0))
    names = ["kv_norm_g", "kv_wd", "kv_latent_g", "kv_wk", "kv_wv"]
    return pl.pallas_call(
        functools.partial(_kv_kernel, lat),
        grid=(bsz, s_len // tm),
        in_specs=[tok(d), tok(MLA_PAD), tok(MLA_PAD)] + [_const_spec(w[n].shape) for n in names],
        out_specs=[tok(nk), tok(nv)],
        out_shape=[jax.ShapeDtypeStruct((bsz, s_len, nk), BF16),
                   jax.ShapeDtypeStruct((bsz, s_len, nv), BF16)],
        compiler_params=_params(("parallel", "parallel")),
        name="mla_kv",
    )(x, cos_t, sin_t, *[w[n] for n in names])


def _b_pre_kernel(q_lora, scale, x_ref, cos_ref, sin_ref, g_ref, w_in_ref, qg_ref, qa_ref, qb_ref,
                  q_ref, qmem_ref):
    h = _rms(x_ref[0], g_ref[...])
    proj = _dot(h, w_in_ref[...])
    qmem_ref[0] = proj[:, q_lora:]
    cq = _rms(proj[:, :q_lora], qg_ref[...]).astype(BF16)
    qa = jnp.dot(cq, qa_ref[...], preferred_element_type=F32)
    qb = jnp.dot(cq, qb_ref[...], preferred_element_type=F32)
    heads = qa.shape[1] // MLA_PAD
    cos_t = jnp.concatenate([cos_ref[0]] * heads, axis=1)
    sin_t = jnp.concatenate([sin_ref[0]] * heads, axis=1)
    q_ref[0] = ((qa * cos_t + qb * sin_t) * scale).astype(q_ref.dtype)


def _b_pre(x, cos_t, sin_t, w, tm):
    bsz, s_len, d = x.shape
    q_lora = w["q_norm_g"].shape[1]
    mem_dim = w["w_in"].shape[1] - q_lora
    nq = w["q_up_a"].shape[1]
    scale = (QK_NOPE + QK_ROPE) ** -0.5 * float(np.log2(np.e))
    tok = lambda n: pl.BlockSpec((1, tm, n), lambda b, j: (b, j, 0))
    names = ["norm_g", "w_in", "q_norm_g", "q_up_a", "q_up_b"]
    return pl.pallas_call(
        functools.partial(_b_pre_kernel, q_lora, scale),
        grid=(bsz, s_len // tm),
        in_specs=[tok(d), tok(MLA_PAD), tok(MLA_PAD)] + [_const_spec(w[n].shape) for n in names],
        out_specs=[tok(nq), tok(mem_dim)],
        out_shape=[jax.ShapeDtypeStruct((bsz, s_len, nq), BF16),
                   jax.ShapeDtypeStruct((bsz, s_len, mem_dim), F32)],
        compiler_params=_params(("parallel", "parallel")),
        name="mla_q",
    )(x, cos_t, sin_t, *[w[n] for n in names])


def _flash_kernel(t, q_ref, k_ref, v_ref, o_ref, m_ref, l_ref, acc_ref, sa_ref, sb_ref):
    i = pl.program_id(2)
    reps = t // 128
    m_ref[...] = jnp.full_like(m_ref, -jnp.inf)
    l_ref[...] = jnp.zeros_like(l_ref)
    acc_ref[...] = jnp.zeros_like(acc_ref)
    low = lax.broadcasted_iota(jnp.int32, (1, 2 * HEAD), 1) < HEAD

    def scores(j, s_ref):
        rows = pl.ds(pl.multiple_of(j * t, t), t)
        for hd in range(2):
            q = q_ref[0, :, hd * MLA_PAD:(hd + 1) * MLA_PAD]
            k = k_ref[0, rows, hd * MLA_PAD:(hd + 1) * MLA_PAD]
            s_ref[hd] = lax.dot_general(q, k, (((1,), (1,)), ((), ())),
                                        preferred_element_type=F32)

    def apply(j, s_ref, diagonal):
        rows = pl.ds(pl.multiple_of(j * t, t), t)
        vpair = v_ref[0, rows, :]
        vhead = (jnp.where(low, vpair, 0), jnp.where(low, 0, vpair))
        if diagonal:
            visible = (lax.broadcasted_iota(jnp.int32, (t, t), 0)
                       >= lax.broadcasted_iota(jnp.int32, (t, t), 1))
        alphas, pvs = [], []
        for hd in range(2):
            s = s_ref[hd]
            if diagonal:
                s = jnp.where(visible, s, -jnp.inf)
            m_prev = m_ref[hd]
            m_new = jnp.maximum(m_prev, jnp.max(s, axis=-1, keepdims=True))
            alpha = jnp.exp2(m_prev - m_new)
            p = jnp.exp2(s - jnp.concatenate([m_new] * reps, axis=1))
            part = p[:, 0:128]
            for c in range(1, reps):
                part = part + p[:, c * 128:(c + 1) * 128]
            l_ref[hd] = alpha * l_ref[hd] + part
            m_ref[hd] = m_new
            pvs.append(jnp.dot(p.astype(BF16), vhead[hd], preferred_element_type=F32))
            alphas.append(alpha)
        acc_ref[...] = acc_ref[...] * jnp.where(low, alphas[0], alphas[1]) + pvs[0] + pvs[1]

    scores(0, sa_ref)

    def two_tiles(n, carry):
        scores(2 * n + 1, sb_ref)
        apply(2 * n, sa_ref, False)
        scores(2 * n + 2, sa_ref)
        apply(2 * n + 1, sb_ref, False)
        return carry

    lax.fori_loop(0, i // 2, two_tiles, 0)

    @pl.when(i % 2 == 1)
    def _():
        scores(i, sb_ref)
        apply(i - 1, sa_ref, False)
        apply(i, sb_ref, True)

    @pl.when(i % 2 == 0)
    def _():
        apply(i, sa_ref, True)

    l0 = jnp.sum(l_ref[0], axis=-1, keepdims=True)
    l1 = jnp.sum(l_ref[1], axis=-1, keepdims=True)
    o_ref[0] = (acc_ref[...] / jnp.where(low, l0, l1)).astype(o_ref.dtype)


def _flash(q, k, v, t):
    bsz, s_len, nq = q.shape
    pairs = nq // (2 * MLA_PAD)
    return pl.pallas_call(
        functools.partial(_flash_kernel, t),
        grid=(bsz, pairs, s_len // t),
        in_specs=[pl.BlockSpec((1, t, 2 * MLA_PAD), lambda b, p, i: (b, i, p)),
                  pl.BlockSpec((1, s_len, 2 * MLA_PAD), lambda b, p, i: (b, 0, p)),
                  pl.BlockSpec((1, s_len, 2 * HEAD), lambda b, p, i: (b, 0, p))],
        out_specs=pl.BlockSpec((1, t, 2 * HEAD), lambda b, p, i: (b, i, p)),
        out_shape=jax.ShapeDtypeStruct((bsz, s_len, pairs * 2 * HEAD), BF16),
        scratch_shapes=[pltpu.VMEM((2, t, 128), F32), pltpu.VMEM((2, t, 128), F32),
                        pltpu.VMEM((t, 2 * HEAD), F32),
                        pltpu.VMEM((2, t, t), F32), pltpu.VMEM((2, t, t), F32)],
        compiler_params=_params(("parallel", "parallel", "arbitrary")),
        name="mla_flash",
    )(q, k, v)


def _row(v):
    return v.reshape(1, -1).astype(F32)


def _pad_rows(w, total, offset):
    out = jnp.zeros((total, w.shape[1]), w.dtype)
    return out.at[offset:offset + w.shape[0]].set(w)


def _block_ones(n):
    h = np.arange(n) // HEAD
    return jnp.asarray(h[:, None] == h[None, :], BF16)


def _rope_cols(w_rope, swap):
    half = QK_ROPE // 2
    t1, t2 = w_rope[:, :half], w_rope[:, half:]
    if swap:
        t1, t2 = t2, t1
    z = lambda n: jnp.zeros((w_rope.shape[0], n), w_rope.dtype)
    return jnp.concatenate([z(QK_NOPE), t1, t2, z(MLA_PAD - QK_NOPE - QK_ROPE)], axis=1)


def _q_up_padded(q_up, heads, swap):
    per = QK_NOPE + QK_ROPE
    blocks = []
    for hd in range(heads):
        wh = q_up[:, hd * per:(hd + 1) * per]
        rope = _rope_cols(wh[:, QK_NOPE:], swap)
        if swap:
            blocks.append(rope)
        else:
            blocks.append(rope.at[:, :QK_NOPE].set(wh[:, :QK_NOPE]))
    return jnp.concatenate(blocks, axis=1).astype(BF16)


def kernel(x, mem, positions, mem_norm_g, a_norm1_g, a_w_in, a_shift_mu, a_decay_up, a_decay_bias, a_aaa_up, a_aaa_bias, a_gate_up, a_k_k, a_k_a, a_r_k, a_lnx_g, a_lnx_b, a_mem_kv, a_w_out, a_norm2_g, a_ffn_gu, a_ffn_down, vres_mu, vres_down, vres_up, vres_bias, kv_norm_g, kv_w_down, kv_latent_g, kv_w_up, b_norm1_g, b_w_in, b_q_norm_g, b_q_up, b_mem_kv, b_w_out, b_norm2_g, b_ffn_gu, b_ffn_down, final_norm_g):
    bsz, s_len, d = x.shape
    n_a = a_w_in.shape[0]
    n_b = b_w_in.shape[0]
    rdim = a_k_k.shape[1]
    assert CHUNK == HEAD and rdim % GL == 0 and s_len % CHUNK == 0
    tm = _tile(s_len, TILES["tm"])
    tc = _tile(s_len, TILES["tc"])
    tq = _tile(s_len, TILES["tq"])

    half = QK_ROPE // 2
    inv_freq = 10000.0 ** (-jnp.arange(half, dtype=F32) / half)
    ang = positions.astype(F32)[..., None] * inv_freq
    cos, sin = jnp.cos(ang), jnp.sin(ang)
    ones = jnp.ones((bsz, s_len, QK_NOPE), F32)
    zpad = jnp.zeros((bsz, s_len, MLA_PAD - QK_NOPE - QK_ROPE), F32)
    cos_t = jnp.concatenate([ones, cos, cos, zpad], axis=-1)
    sin_t = jnp.concatenate([0 * ones, -sin, sin, zpad], axis=-1)

    memkv = _memkv(mem, _row(mem_norm_g),
                   jnp.concatenate(list(a_mem_kv) + list(b_mem_kv), axis=1).astype(BF16))
    bd768 = _block_ones(rdim)

    v_first = None
    for i in range(n_a):
        w = dict(norm_g=_row(a_norm1_g[i]), w_in=a_w_in[i].astype(BF16), mu=_row(a_shift_mu[i]),
                 decay_up=_pad_rows(a_decay_up[i], 128, 0).astype(BF16),
                 decay_bias=_row(a_decay_bias[i]),
                 aaa_up=_pad_rows(a_aaa_up[i], 128, a_decay_up.shape[1]).astype(BF16),
                 aaa_bias=_row(a_aaa_bias[i]), gate_up=a_gate_up[i].astype(BF16),
                 k_k=_row(a_k_k[i]), k_a=_row(a_k_a[i]), bd768=bd768)
        if i > 0:
            w.update(vres_mu=_row(vres_mu[i - 1]),
                     vres_down=jnp.pad(vres_down[i - 1], ((0, 0), (0, 128 - vres_down.shape[2]))).astype(BF16),
                     vres_up=_pad_rows(vres_up[i - 1], 128, 0).astype(BF16),
                     vres_bias=_row(vres_bias[i - 1]))
        r, lw, k, v, a, b, gate, qmem = _a_pre(x, v_first if i > 0 else None, w, tm)
        if i == 0:
            v_first = v
        y_mix = _scan(r, lw, k, v, a, b, gate, _row(a_lnx_g[i]), _row(a_lnx_b[i]),
                      _row(a_r_k[i]), tc)
        wp = dict(w_out=a_w_out[i].astype(BF16), norm2_g=_row(a_norm2_g[i]),
                  ffn_gu=a_ffn_gu[i].astype(BF16), ffn_down=a_ffn_down[i].astype(BF16))
        x = _post(x, y_mix, qmem, memkv, i, wp, None, tm)

    lat = kv_latent_g.shape[0]
    heads = b_q_up.shape[2] // (QK_NOPE + QK_ROPE)
    per_kv = kv_w_up.shape[1] // heads
    wk_blocks, wv_blocks = [], []
    for hd in range(heads):
        blk = kv_w_up[:, hd * per_kv:(hd + 1) * per_kv]
        wk_blocks.append(jnp.pad(blk[:, :QK_NOPE], ((0, 0), (0, MLA_PAD - QK_NOPE))))
        wv_blocks.append(blk[:, QK_NOPE:])
    wkv = dict(kv_norm_g=_row(kv_norm_g),
               kv_wd=jnp.concatenate([kv_w_down[:, :lat], _rope_cols(kv_w_down[:, lat:], False),
                                      _rope_cols(kv_w_down[:, lat:], True)], axis=1).astype(BF16),
               kv_latent_g=_row(kv_latent_g),
               kv_wk=jnp.concatenate(wk_blocks, axis=1).astype(BF16),
               kv_wv=jnp.concatenate(wv_blocks, axis=1).astype(BF16))
    k_all, v_all = _kv(x, cos_t, sin_t, wkv, tm)

    for j in range(n_b):
        wq = dict(norm_g=_row(b_norm1_g[j]), w_in=b_w_in[j].astype(BF16),
                  q_norm_g=_row(b_q_norm_g[j]),
                  q_up_a=_q_up_padded(b_q_up[j], heads, False),
                  q_up_b=_q_up_padded(b_q_up[j], heads, True))
        q, qmem = _b_pre(x, cos_t, sin_t, wq, tm)
        o = _flash(q, k_all, v_all, tq)
        wp = dict(w_out=b_w_out[j].astype(BF16), norm2_g=_row(b_norm2_g[j]),
                  ffn_gu=b_ffn_gu[j].astype(BF16), ffn_down=b_ffn_down[j].astype(BF16))
        x = _post(x, o, qmem, memkv, n_a + j, wp,
                  _row(final_norm_g) if j == n_b - 1 else None, tm)
    return x
```

```python
import functools

import numpy as np
import jax
import jax.numpy as jnp
from jax import lax
from jax.experimental import pallas as pl
from jax.experimental.pallas import tpu as pltpu

F32 = jnp.float32
BF16 = jnp.bfloat16

HEAD = 64
CHUNK = 64
GROUP = 4
GL = GROUP * HEAD
NORM_EPS = 1e-6
LNX_EPS = 64e-5
QK_NOPE = 64
QK_ROPE = 32
MLA_PAD = 128
VMEM_LIMIT = 56 * 1024 * 1024
TILES = dict(tm=512, tc=512, tq=1024)


def _dot(a, b):
    return jnp.dot(a.astype(BF16), b.astype(BF16), preferred_element_type=F32)


def _dot_nt(a, b):
    return lax.dot_general(a.astype(BF16), b.astype(BF16), (((1,), (1,)), ((), ())),
                           preferred_element_type=F32)


def _dot_tn(a, b):
    return lax.dot_general(a.astype(BF16), b.astype(BF16), (((0,), (0,)), ((), ())),
                           preferred_element_type=F32)


def _split(x, terms):
    out = []
    for _ in range(terms):
        p = x.astype(BF16)
        out.append(p)
        x = x - p.astype(F32)
    return out


def _dot_ones_rhs(x, ones_bf16, terms=2):
    acc = None
    for p in _split(x, terms):
        d = jnp.dot(p, ones_bf16, preferred_element_type=F32)
        acc = d if acc is None else acc + d
    return acc


def _dot_ones_lhs(ones_bf16, x, terms=3):
    acc = None
    for p in _split(x, terms):
        d = jnp.dot(ones_bf16, p, preferred_element_type=F32)
        acc = d if acc is None else acc + d
    return acc


def _rms(x, g, eps=NORM_EPS):
    ms = jnp.mean(x * x, axis=-1, keepdims=True)
    return x * lax.rsqrt(ms + eps) * g


def _sigmoid(x):
    return 1.0 / (1.0 + jnp.exp(-x))


def _shift_rows(x, carry_ref):
    tm = x.shape[0]
    first = lax.broadcasted_iota(jnp.int32, (tm, 1), 0) == 0
    prev = jnp.where(first, carry_ref[0:1, :], pltpu.roll(x, 1, 0))
    carry_ref[0:1, :] = x[tm - 1:tm, :]
    return prev


def _const_spec(shape):
    nd = len(shape)
    return pl.BlockSpec(shape, lambda *_: (0,) * nd, pipeline_mode=pl.Buffered(1))


def _params(sem):
    return pltpu.CompilerParams(dimension_semantics=sem, vmem_limit_bytes=VMEM_LIMIT)


def _tile(n, pref):
    t = min(n, pref)
    assert n % t == 0, (n, t)
    return t


def _memkv_kernel(mem_ref, g_ref, w_ref, o_ref):
    mn = _rms(mem_ref[0], g_ref[...])
    o_ref[0] = _dot(mn, w_ref[...]).astype(o_ref.dtype)


def _memkv(mem, g, w_all):
    bsz, n_mem, d = mem.shape
    n = w_all.shape[1]
    return pl.pallas_call(
        _memkv_kernel,
        grid=(bsz,),
        in_specs=[pl.BlockSpec((1, n_mem, d), lambda b: (b, 0, 0)),
                  _const_spec((1, d)), _const_spec((d, n))],
        out_specs=pl.BlockSpec((1, n_mem, n), lambda b: (b, 0, 0)),
        out_shape=jax.ShapeDtypeStruct((bsz, n_mem, n), BF16),
        compiler_params=_params(("parallel",)),
        name="memkv",
    )(mem, g, w_all)


def _a_pre_kernel(has_vres, rdim, rwkv_in, *refs):
    it = iter(refs)
    x_ref = next(it)
    vfirst_ref = next(it) if has_vres else None
    g_ref, w_in_ref, mu_ref = next(it), next(it), next(it)
    dup_ref, dbias_ref, aup_ref, abias_ref, gup_ref = (next(it) for _ in range(5))
    kk_ref, ka_ref, bd_ref = next(it), next(it), next(it)
    if has_vres:
        vmu_ref, vdown_ref, vup_ref, vbias_ref = (next(it) for _ in range(4))
    r_ref, lw_ref, k_ref, v_ref, a_ref, b_ref, gate_ref, qmem_ref = (next(it) for _ in range(8))
    carry_p = next(it)
    carry_h = next(it) if has_vres else None

    @pl.when(pl.program_id(1) == 0)
    def _():
        carry_p[...] = jnp.zeros_like(carry_p)
        if has_vres:
            carry_h[...] = jnp.zeros_like(carry_h)

    h = _rms(x_ref[0], g_ref[...])
    proj = _dot(h, w_in_ref[...])
    qmem_ref[0] = proj[:, rwkv_in:]
    p = proj[:, :rwkv_in]
    pm = p + (_shift_rows(p, carry_p) - p) * mu_ref[...]

    r = pm[:, 0:rdim]
    k = pm[:, rdim:2 * rdim]
    v = pm[:, 2 * rdim:3 * rdim]
    lo = pm[:, 3 * rdim:3 * rdim + 128]
    g_lo = pm[:, 3 * rdim + 128:]

    z = dbias_ref[...] + _dot(jnp.tanh(lo), dup_ref[...])
    nz = -z
    softplus = jnp.maximum(nz, 0.0) + jnp.log1p(jnp.exp(-jnp.abs(nz)))
    log_w = -softplus - 0.5
    lw_ref[0] = -jnp.exp(log_w)
    lr = _sigmoid(abias_ref[...] + _dot(lo, aup_ref[...]))
    gate_ref[0] = _dot(_sigmoid(g_lo), gup_ref[...])

    if has_vres:
        hv = h + (_shift_rows(h, carry_h) - h) * vmu_ref[...]
        t = _dot(hv, vdown_ref[...])
        sg = _sigmoid(vbias_ref[...] + _dot(t, vup_ref[...]))
        v = v + (vfirst_ref[0] - v) * sg

    kkr = k * kk_ref[...]
    norm = jnp.sqrt(_dot_ones_rhs(kkr * kkr, bd_ref[...]))
    kk = kkr / jnp.maximum(norm, 1e-12)
    r_ref[0] = r
    k_ref[0] = k * (1.0 + (lr - 1.0) * ka_ref[...])
    v_ref[0] = v
    a_ref[0] = -kk
    b_ref[0] = kk * lr


def _a_pre(x, vfirst, w, tm):
    bsz, s_len, d = x.shape
    rdim = w["k_k"].shape[1]
    n_in = w["w_in"].shape[1]
    rwkv_in = w["mu"].shape[1]
    mem_dim = n_in - rwkv_in
    has_vres = vfirst is not None
    tok = lambda n: pl.BlockSpec((1, tm, n), lambda b, j: (b, j, 0))
    args, specs = [x], [tok(d)]
    if has_vres:
        args.append(vfirst)
        specs.append(tok(rdim))
    names = ["norm_g", "w_in", "mu", "decay_up", "decay_bias", "aaa_up", "aaa_bias", "gate_up",
             "k_k", "k_a", "bd768"]
    if has_vres:
        names += ["vres_mu", "vres_down", "vres_up", "vres_bias"]
    for n in names:
        args.append(w[n])
        specs.append(_const_spec(w[n].shape))
    scratch = [pltpu.VMEM((8, rwkv_in), F32)]
    if has_vres:
        scratch.append(pltpu.VMEM((8, d), F32))
    out_shape = [jax.ShapeDtypeStruct((bsz, s_len, rdim), F32)] * 7 + \
                [jax.ShapeDtypeStruct((bsz, s_len, mem_dim), F32)]
    out_specs = [tok(rdim)] * 7 + [tok(mem_dim)]
    return pl.pallas_call(
        functools.partial(_a_pre_kernel, has_vres, rdim, rwkv_in),
        grid=(bsz, s_len // tm),
        in_specs=specs, out_specs=out_specs, out_shape=out_shape,
        scratch_shapes=scratch,
        compiler_params=_params(("parallel", "arbitrary")),
        name="a_pre_vres" if has_vres else "a_pre",
    )(*args)


def _stack_heads(x_bf16, hm_ref):
    return jnp.concatenate([x_bf16 * hm_ref[h] for h in range(GROUP)], axis=0)


def _scan_prepare(ins, consts):
    tri, hm_ref, mask2, icat, bd = consts
    c = CHUNK
    gc = GROUP * c
    each = range(len(ins))
    bd_b = bd[...]
    r, lw, k, a, b = zip(*ins)

    cl = [_dot_ones_lhs(tri[...], lw[g]) for g in each]
    w_end = [jnp.exp(cl[g][c - 1:c, :]) for g in each]
    e_inc = [jnp.exp(cl[g]) for g in each]
    e_exc = [jnp.exp(cl[g] - lw[g]) for g in each]
    e_inv = [jnp.exp(-cl[g]) for g in each]
    kt = [k[g] * e_inv[g] for g in each]
    bt = [b[g] * e_inv[g] for g in each]
    ar = [jnp.concatenate([a[g] * e_exc[g], r[g] * e_inc[g]], axis=0).astype(BF16) for g in each]
    bks = [jnp.concatenate([_stack_heads(bt[g].astype(BF16), hm_ref),
                            _stack_heads(kt[g].astype(BF16), hm_ref)], axis=0) for g in each]
    m = [_dot_nt(ar[g], bks[g]) * mask2[...] for g in each]
    l_ab = [m[g][:c, :gc] for g in each]
    bkh = [jnp.concatenate([bt[g] * w_end[g], kt[g] * w_end[g]], axis=0).astype(BF16) for g in each]

    def blockdiag(p_bf16):
        return jnp.concatenate([p_bf16] * GROUP, axis=0) * bd_b

    pw = [l_ab[g].astype(BF16) for g in each]
    pw = [jnp.dot(pw[g], blockdiag(pw[g]), preferred_element_type=F32) for g in each]
    inv = [icat[...] + l_ab[g] for g in each]
    steps = int(np.log2(c)) - 1
    for it in range(steps):
        pbd = [blockdiag(pw[g].astype(BF16)) for g in each]
        if it + 1 < steps:
            both = [jnp.dot(jnp.concatenate([inv[g], pw[g]], axis=0).astype(BF16), pbd[g],
                            preferred_element_type=F32) for g in each]
            inv = [inv[g] + both[g][:c] for g in each]
            pw = [both[g][c:] for g in each]
        else:
            inv = [inv[g] + jnp.dot(inv[g].astype(BF16), pbd[g], preferred_element_type=F32)
                   for g in each]
    return (ar, [m[g][:c, gc:].astype(BF16) for g in each], [m[g][c:, :].astype(BF16) for g in each],
            [inv[g].astype(BF16) for g in each], bkh, w_end)


def _scan_advance(pre, vb, states, hm_ref, bd_f):
    c = CHUNK
    each = range(len(states))
    ar, l_ak, m_r, inv, bkh, w_end = pre
    ars = [_dot_nt(ar[g], states[g]) for g in each]
    vs = [_stack_heads(vb[g], hm_ref) for g in each]
    z = [ars[g][:c] + jnp.dot(l_ak[g], vs[g], preferred_element_type=F32) for g in each]
    u = [jnp.dot(inv[g], _stack_heads(z[g].astype(BF16), hm_ref), preferred_element_type=F32)
         for g in each]
    ub = [u[g].astype(BF16) for g in each]
    y = [ars[g][c:] + jnp.dot(m_r[g], jnp.concatenate([_stack_heads(ub[g], hm_ref), vs[g]], axis=0),
                              preferred_element_type=F32) for g in each]
    s_new = [states[g] * w_end[g]
             + bd_f * _dot_tn(jnp.concatenate([ub[g], vb[g]], axis=0), bkh[g]) for g in each]
    return y, s_new


def _scan_kernel(n_groups, n_chunks, per_iter, r_ref, lw_ref, k_ref, v_ref, a_ref, b_ref, gate_ref,
                 lnxg_ref, lnxb_ref, rk_ref, tri_ref, hm_ref, mask2_ref, icat_ref, bd_ref,
                 o_ref, state_ref, ar_scr, lak_scr, mr_scr, inv_scr, bkh_scr, wend_scr, y_scr):
    @pl.when(pl.program_id(1) == 0)
    def _():
        state_ref[...] = jnp.zeros_like(state_ref)

    c = CHUNK
    gc = GROUP * c
    consts = (tri_ref, hm_ref, mask2_ref, icat_ref, bd_ref)
    groups = range(n_groups)
    lanes = [slice(g * GL, (g + 1) * GL) for g in groups]
    cat = [slice(g * gc, (g + 1) * gc) for g in groups]
    cat2 = [slice(g * 2 * gc, (g + 1) * 2 * gc) for g in groups]

    def rows(ci, n):
        return pl.ds(pl.multiple_of(ci * n, n), n)

    def prepare(pi, carry):
        where = [(pi * per_iter + cc, g) for cc in range(per_iter) for g in groups]
        ins = [tuple(ref[0, rows(ci, c), lanes[g]] for ref in (r_ref, lw_ref, k_ref, a_ref, b_ref))
               for ci, g in where]
        ar, l_ak, m_r, inv, bkh, w_end = _scan_prepare(ins, consts)
        for n, (ci, g) in enumerate(where):
            ar_scr[rows(ci, 2 * c), lanes[g]] = ar[n]
            lak_scr[rows(ci, c), cat[g]] = l_ak[n]
            mr_scr[rows(ci, c), cat2[g]] = m_r[n]
            inv_scr[rows(ci, c), cat[g]] = inv[n]
            bkh_scr[rows(ci, 2 * c), lanes[g]] = bkh[n]
            wend_scr[ci, :, lanes[g]] = w_end[n]
        return carry

    lax.fori_loop(0, n_chunks // per_iter, prepare, 0)

    bd_f = bd_ref[...].astype(F32)

    def advance(ci, carry):
        pre = ([ar_scr[rows(ci, 2 * c), lanes[g]] for g in groups],
               [lak_scr[rows(ci, c), cat[g]] for g in groups],
               [mr_scr[rows(ci, c), cat2[g]] for g in groups],
               [inv_scr[rows(ci, c), cat[g]] for g in groups],
               [bkh_scr[rows(ci, 2 * c), lanes[g]] for g in groups],
               [wend_scr[ci, :, lanes[g]] for g in groups])
        vb = [v_ref[0, rows(ci, c), lanes[g]].astype(BF16) for g in groups]
        y, s_new = _scan_advance(pre, vb, [state_ref[g] for g in groups], hm_ref, bd_f)
        for g in groups:
            state_ref[g] = s_new[g]
            y_scr[rows(ci, c), lanes[g]] = y[g]
        return carry

    lax.fori_loop(0, n_chunks, advance, 0)

    bd_b = bd_ref[...]
    for g in groups:
        ln = lanes[g]
        y = y_scr[:, ln]
        mean = _dot_ones_rhs(y, bd_b) * (1.0 / HEAD)
        d = y - mean
        var = _dot_ones_rhs(d * d, bd_b) * (1.0 / HEAD)
        v = v_ref[0, :, ln]
        bonus = _dot_ones_rhs(r_ref[0, :, ln] * k_ref[0, :, ln] * rk_ref[:, ln], bd_b) * v
        gn = d * lax.rsqrt(var + LNX_EPS) * lnxg_ref[:, ln] + lnxb_ref[:, ln]
        o_ref[0, :, ln] = ((gn + bonus) * gate_ref[0, :, ln]).astype(o_ref.dtype)


def _scan_consts():
    c, g = CHUNK, GROUP
    gc = g * c
    i = np.arange(c)[:, None]
    j = np.arange(gc)[None, :] % c
    strict = (j < i).astype(np.float32)
    incl = (j <= i).astype(np.float32)
    mask2 = np.concatenate([np.concatenate([strict, strict], 1),
                            np.concatenate([incl, incl], 1)], 0)
    icat = (j == i).astype(np.float32)
    tri = (np.arange(c)[None, :] <= np.arange(c)[:, None]).astype(np.float32)
    lane_head = np.arange(GL) // HEAD
    hm = (lane_head[None, None, :] == np.arange(g)[:, None, None]).astype(np.float32)
    bd = (lane_head[:, None] == lane_head[None, :]).astype(np.float32)
    return (jnp.asarray(tri, BF16), jnp.asarray(hm, BF16), jnp.asarray(mask2, F32),
            jnp.asarray(icat, F32), jnp.asarray(bd, BF16))


def _scan(r, lw, k, v, a, b, gate, lnx_g, lnx_b, r_k, tc):
    bsz, s_len, rdim = r.shape
    n_groups = rdim // GL
    n_chunks = tc // CHUNK
    per_iter = 4 if n_chunks % 4 == 0 else 1
    gc = GROUP * CHUNK
    tok = pl.BlockSpec((1, tc, rdim), lambda bi, j: (bi, j, 0))
    consts = _scan_consts()
    small = [lnx_g, lnx_b, r_k] + list(consts)
    return pl.pallas_call(
        functools.partial(_scan_kernel, n_groups, n_chunks, per_iter),
        grid=(bsz, s_len // tc),
        in_specs=[tok] * 7 + [_const_spec(t.shape) for t in small],
        out_specs=tok,
        out_shape=jax.ShapeDtypeStruct((bsz, s_len, rdim), BF16),
        scratch_shapes=[pltpu.VMEM((n_groups, GL, GL), F32),
                        pltpu.VMEM((2 * tc, rdim), BF16),
                        pltpu.VMEM((tc, n_groups * gc), BF16),
                        pltpu.VMEM((tc, n_groups * 2 * gc), BF16),
                        pltpu.VMEM((tc, n_groups * gc), BF16),
                        pltpu.VMEM((2 * tc, rdim), BF16),
                        pltpu.VMEM((n_chunks, 1, rdim), F32),
                        pltpu.VMEM((tc, rdim), F32)],
        compiler_params=_params(("parallel", "arbitrary")),
        name="rwkv_scan",
    )(r, lw, k, v, a, b, gate, *small)


def _post_kernel(mem_heads, ffn_tiles, final, x_ref, mix_ref, qmem_ref, mk_ref, mv_ref, hm_ref,
                 w_out_ref, g2_ref, w_gu_ref, w_dn_ref, gf_ref, o_ref):
    mix_dim = mix_ref.shape[2]
    fh = w_dn_ref.shape[0]
    th = fh // ffn_tiles

    qm = (qmem_ref[0] * (HEAD ** -0.5)).astype(BF16)
    mk = mk_ref[0]
    mv = mv_ref[0]
    m = None
    for hd in range(mem_heads):
        s = _dot_nt(qm * hm_ref[hd], mk)
        s = s - jnp.max(s, axis=-1, keepdims=True)
        e = jnp.exp(s)
        p = e / jnp.sum(e, axis=-1, keepdims=True)
        o = _dot(p, mv) * hm_ref[hd].astype(F32)
        m = o if m is None else m + o

    x1 = x_ref[0] + _dot(mix_ref[0], w_out_ref[0:mix_dim, :]) + _dot(m, w_out_ref[mix_dim:, :])
    h2 = _rms(x1, g2_ref[...]).astype(BF16)
    acc = None
    for t in range(ffn_tiles):
        gt = jnp.dot(h2, w_gu_ref[:, t * th:(t + 1) * th], preferred_element_type=F32)
        ut = jnp.dot(h2, w_gu_ref[:, fh + t * th:fh + (t + 1) * th], preferred_element_type=F32)
        act = gt * _sigmoid(gt) * ut
        dt = _dot(act, w_dn_ref[t * th:(t + 1) * th, :])
        acc = dt if acc is None else acc + dt
    x2 = x1 + acc
    if final:
        x2 = _rms(x2, gf_ref[...])
    o_ref[0] = x2


def _post(x, mix, qmem, memkv, layer, w, final_g, tm):
    bsz, s_len, d = x.shape
    mix_dim = mix.shape[2]
    mem_dim = qmem.shape[2]
    n_mem = memkv.shape[1]
    mem_heads = mem_dim // HEAD
    fh = w["ffn_down"].shape[0]
    ffn_tiles = 2 if fh % 256 == 0 else 1
    lane_head = np.arange(mem_dim) // HEAD
    hm = jnp.asarray((lane_head[None, None, :] == np.arange(mem_heads)[:, None, None]), BF16)
    tok = lambda n: pl.BlockSpec((1, tm, n), lambda b, j: (b, j, 0))
    final = final_g is not None
    gf = final_g if final else w["norm2_g"]
    return pl.pallas_call(
        functools.partial(_post_kernel, mem_heads, ffn_tiles, final),
        grid=(bsz, s_len // tm),
        in_specs=[tok(d), tok(mix_dim), tok(mem_dim),
                  pl.BlockSpec((1, n_mem, mem_dim), lambda b, j: (b, 0, 2 * layer)),
                  pl.BlockSpec((1, n_mem, mem_dim), lambda b, j: (b, 0, 2 * layer + 1)),
                  _const_spec(hm.shape), _const_spec(w["w_out"].shape),
                  _const_spec(w["norm2_g"].shape), _const_spec(w["ffn_gu"].shape),
                  _const_spec(w["ffn_down"].shape), _const_spec(gf.shape)],
        out_specs=tok(d),
        out_shape=jax.ShapeDtypeStruct((bsz, s_len, d), F32),
        compiler_params=_params(("parallel", "parallel")),
        name="post_final" if final else "post",
    )(x, mix, qmem, memkv, memkv, hm, w["w_out"], w["norm2_g"], w["ffn_gu"], w["ffn_down"], gf)


def _kv_kernel(lat, x_ref, cos_ref, sin_ref, g_ref, wd_ref, lg_ref, wk_ref, wv_ref, k_ref, v_ref):
    hk = _rms(x_ref[0], g_ref[...])
    ckr = _dot(hk, wd_ref[...])
    ckv = _rms(ckr[:, :lat], lg_ref[...])
    krot = ckr[:, lat:lat + MLA_PAD] * cos_ref[0] + ckr[:, lat + MLA_PAD:] * sin_ref[0]
    kn = _dot(ckv, wk_ref[...])
    heads = kn.shape[1] // MLA_PAD
    k_ref[0] = (kn + jnp.concatenate([krot] * heads, axis=1)).astype(k_ref.dtype)
    v_ref[0] = _dot(ckv, wv_ref[...]).astype(v_ref.dtype)


def _kv(x, cos_t, sin_t, w, tm):
    bsz, s_len, d = x.shape
    lat = w["kv_latent_g"].shape[1]
    nk = w["kv_wk"].shape[1]
    nv = w["kv_wv"].shape[1]
    tok = lambda n: pl.BlockSpec((1, tm, n), lambda b, j: (b, j, 0))
    names = ["kv_norm_g", "kv_wd", "kv_latent_g", "kv_wk", "kv_wv"]
    return pl.pallas_call(
        functools.partial(_kv_kernel, lat),
        grid=(bsz, s_len // tm),
        in_specs=[tok(d), tok(MLA_PAD), tok(MLA_PAD)] + [_const_spec(w[n].shape) for n in names],
        out_specs=[tok(nk), tok(nv)],
        out_shape=[jax.ShapeDtypeStruct((bsz, s_len, nk), BF16),
                   jax.ShapeDtypeStruct((bsz, s_len, nv), BF16)],
        compiler_params=_params(("parallel", "parallel")),
        name="mla_kv",
    )(x, cos_t, sin_t, *[w[n] for n in names])


def _b_pre_kernel(q_lora, scale, x_ref, cos_ref, sin_ref, g_ref, w_in_ref, qg_ref, qa_ref, qb_ref,
                  q_ref, qmem_ref):
    h = _rms(x_ref[0], g_ref[...])
    proj = _dot(h, w_in_ref[...])
    qmem_ref[0] = proj[:, q_lora:]
    cq = _rms(proj[:, :q_lora], qg_ref[...]).astype(BF16)
    qa = jnp.dot(cq, qa_ref[...], preferred_element_type=F32)
    qb = jnp.dot(cq, qb_ref[...], preferred_element_type=F32)
    heads = qa.shape[1] // MLA_PAD
    cos_t = jnp.concatenate([cos_ref[0]] * heads, axis=1)
    sin_t = jnp.concatenate([sin_ref[0]] * heads, axis=1)
    q_ref[0] = ((qa * cos_t + qb * sin_t) * scale).astype(q_ref.dtype)


def _b_pre(x, cos_t, sin_t, w, tm):
    bsz, s_len, d = x.shape
    q_lora = w["q_norm_g"].shape[1]
    mem_dim = w["w_in"].shape[1] - q_lora
    nq = w["q_up_a"].shape[1]
    scale = (QK_NOPE + QK_ROPE) ** -0.5 * float(np.log2(np.e))
    tok = lambda n: pl.BlockSpec((1, tm, n), lambda b, j: (b, j, 0))
    names = ["norm_g", "w_in", "q_norm_g", "q_up_a", "q_up_b"]
    return pl.pallas_call(
        functools.partial(_b_pre_kernel, q_lora, scale),
        grid=(bsz, s_len // tm),
        in_specs=[tok(d), tok(MLA_PAD), tok(MLA_PAD)] + [_const_spec(w[n].shape) for n in names],
        out_specs=[tok(nq), tok(mem_dim)],
        out_shape=[jax.ShapeDtypeStruct((bsz, s_len, nq), BF16),
                   jax.ShapeDtypeStruct((bsz, s_len, mem_dim), F32)],
        compiler_params=_params(("parallel", "parallel")),
        name="mla_q",
    )(x, cos_t, sin_t, *[w[n] for n in names])


def _flash_kernel(t, q_ref, k_ref, v_ref, o_ref, m_ref, l_ref, acc_ref, sa_ref, sb_ref):
    i = pl.program_id(2)
    reps = t // 128
    m_ref[...] = jnp.full_like(m_ref, -jnp.inf)
    l_ref[...] = jnp.zeros_like(l_ref)
    acc_ref[...] = jnp.zeros_like(acc_ref)
    low = lax.broadcasted_iota(jnp.int32, (1, 2 * HEAD), 1) < HEAD

    def scores(j, s_ref):
        rows = pl.ds(pl.multiple_of(j * t, t), t)
        for hd in range(2):
            q = q_ref[0, :, hd * MLA_PAD:(hd + 1) * MLA_PAD]
            k = k_ref[0, rows, hd * MLA_PAD:(hd + 1) * MLA_PAD]
            s_ref[hd] = lax.dot_general(q, k, (((1,), (1,)), ((), ())),
                                        preferred_element_type=F32)

    def apply(j, s_ref, diagonal):
        rows = pl.ds(pl.multiple_of(j * t, t), t)
        vpair = v_ref[0, rows, :]
        vhead = (jnp.where(low, vpair, 0), jnp.where(low, 0, vpair))
        if diagonal:
            visible = (lax.broadcasted_iota(jnp.int32, (t, t), 0)
                       >= lax.broadcasted_iota(jnp.int32, (t, t), 1))
        alphas, pvs = [], []
        for hd in range(2):
            s = s_ref[hd]
            if diagonal:
                s = jnp.where(visible, s, -jnp.inf)
            m_prev = m_ref[hd]
            m_new = jnp.maximum(m_prev, jnp.max(s, axis=-1, keepdims=True))
            alpha = jnp.exp2(m_prev - m_new)
            p = jnp.exp2(s - jnp.concatenate([m_new] * reps, axis=1))
            part = p[:, 0:128]
            for c in range(1, reps):
                part = part + p[:, c * 128:(c + 1) * 128]
            l_ref[hd] = alpha * l_ref[hd] + part
            m_ref[hd] = m_new
            pvs.append(jnp.dot(p.astype(BF16), vhead[hd], preferred_element_type=F32))
            alphas.append(alpha)
        acc_ref[...] = acc_ref[...] * jnp.where(low, alphas[0], alphas[1]) + pvs[0] + pvs[1]

    scores(0, sa_ref)

    def two_tiles(n, carry):
        scores(2 * n + 1, sb_ref)
        apply(2 * n, sa_ref, False)
        scores(2 * n + 2, sa_ref)
        apply(2 * n + 1, sb_ref, False)
        return carry

    lax.fori_loop(0, i // 2, two_tiles, 0)

    @pl.when(i % 2 == 1)
    def _():
        scores(i, sb_ref)
        apply(i - 1, sa_ref, False)
        apply(i, sb_ref, True)

    @pl.when(i % 2 == 0)
    def _():
        apply(i, sa_ref, True)

    l0 = jnp.sum(l_ref[0], axis=-1, keepdims=True)
    l1 = jnp.sum(l_ref[1], axis=-1, keepdims=True)
    o_ref[0] = (acc_ref[...] / jnp.where(low, l0, l1)).astype(o_ref.dtype)


def _flash(q, k, v, t):
    bsz, s_len, nq = q.shape
    pairs = nq // (2 * MLA_PAD)
    return pl.pallas_call(
        functools.partial(_flash_kernel, t),
        grid=(bsz, pairs, s_len // t),
        in_specs=[pl.BlockSpec((1, t, 2 * MLA_PAD), lambda b, p, i: (b, i, p)),
                  pl.BlockSpec((1, s_len, 2 * MLA_PAD), lambda b, p, i: (b, 0, p)),
                  pl.BlockSpec((1, s_len, 2 * HEAD), lambda b, p, i: (b, 0, p))],
        out_specs=pl.BlockSpec((1, t, 2 * HEAD), lambda b, p, i: (b, i, p)),
        out_shape=jax.ShapeDtypeStruct((bsz, s_len, pairs * 2 * HEAD), BF16),
        scratch_shapes=[pltpu.VMEM((2, t, 128), F32), pltpu.VMEM((2, t, 128), F32),
                        pltpu.VMEM((t, 2 * HEAD), F32),
                        pltpu.VMEM((2, t, t), F32), pltpu.VMEM((2, t, t), F32)],
        compiler_params=_params(("parallel", "parallel", "arbitrary")),
        name="mla_flash",
    )(q, k, v)


def _row(v):
    return v.reshape(1, -1).astype(F32)


def _pad_rows(w, total, offset):
    out = jnp.zeros((total, w.shape[1]), w.dtype)
    return out.at[offset:offset + w.shape[0]].set(w)


def _block_ones(n):
    h = np.arange(n) // HEAD
    return jnp.asarray(h[:, None] == h[None, :], BF16)


def _rope_cols(w_rope, swap):
    half = QK_ROPE // 2
    t1, t2 = w_rope[:, :half], w_rope[:, half:]
    if swap:
        t1, t2 = t2, t1
    z = lambda n: jnp.zeros((w_rope.shape[0], n), w_rope.dtype)
    return jnp.concatenate([z(QK_NOPE), t1, t2, z(MLA_PAD - QK_NOPE - QK_ROPE)], axis=1)


def _q_up_padded(q_up, heads, swap):
    per = QK_NOPE + QK_ROPE
    blocks = []
    for hd in range(heads):
        wh = q_up[:, hd * per:(hd + 1) * per]
        rope = _rope_cols(wh[:, QK_NOPE:], swap)
        if swap:
            blocks.append(rope)
        else:
            blocks.append(rope.at[:, :QK_NOPE].set(wh[:, :QK_NOPE]))
    return jnp.concatenate(blocks, axis=1).astype(BF16)


def kernel(x, mem, positions, mem_norm_g, a_norm1_g, a_w_in, a_shift_mu, a_decay_up, a_decay_bias, a_aaa_up, a_aaa_bias, a_gate_up, a_k_k, a_k_a, a_r_k, a_lnx_g, a_lnx_b, a_mem_kv, a_w_out, a_norm2_g, a_ffn_gu, a_ffn_down, vres_mu, vres_down, vres_up, vres_bias, kv_norm_g, kv_w_down, kv_latent_g, kv_w_up, b_norm1_g, b_w_in, b_q_norm_g, b_q_up, b_mem_kv, b_w_out, b_norm2_g, b_ffn_gu, b_ffn_down, final_norm_g):
    bsz, s_len, d = x.shape
    n_a = a_w_in.shape[0]
    n_b = b_w_in.shape[0]
    rdim = a_k_k.shape[1]
    assert CHUNK == HEAD and rdim % GL == 0 and s_len % CHUNK == 0
    tm = _tile(s_len, TILES["tm"])
    tc = _tile(s_len, TILES["tc"])
    tq = _tile(s_len, TILES["tq"])

    half = QK_ROPE // 2
    inv_freq = 10000.0 ** (-jnp.arange(half, dtype=F32) / half)
    ang = positions.astype(F32)[..., None] * inv_freq
    cos, sin = jnp.cos(ang), jnp.sin(ang)
    ones = jnp.ones((bsz, s_len, QK_NOPE), F32)
    zpad = jnp.zeros((bsz, s_len, MLA_PAD - QK_NOPE - QK_ROPE), F32)
    cos_t = jnp.concatenate([ones, cos, cos, zpad], axis=-1)
    sin_t = jnp.concatenate([0 * ones, -sin, sin, zpad], axis=-1)

    memkv = _memkv(mem, _row(mem_norm_g),
                   jnp.concatenate(list(a_mem_kv) + list(b_mem_kv), axis=1).astype(BF16))
    bd768 = _block_ones(rdim)

    v_first = None
    for i in range(n_a):
        w = dict(norm_g=_row(a_norm1_g[i]), w_in=a_w_in[i].astype(BF16), mu=_row(a_shift_mu[i]),
                 decay_up=_pad_rows(a_decay_up[i], 128, 0).astype(BF16),
                 decay_bias=_row(a_decay_bias[i]),
                 aaa_up=_pad_rows(a_aaa_up[i], 128, a_decay_up.shape[1]).astype(BF16),
                 aaa_bias=_row(a_aaa_bias[i]), gate_up=a_gate_up[i].astype(BF16),
                 k_k=_row(a_k_k[i]), k_a=_row(a_k_a[i]), bd768=bd768)
        if i > 0:
            w.update(vres_mu=_row(vres_mu[i - 1]),
                     vres_down=jnp.pad(vres_down[i - 1], ((0, 0), (0, 128 - vres_down.shape[2]))).astype(BF16),
                     vres_up=_pad_rows(vres_up[i - 1], 128, 0).astype(BF16),
                     vres_bias=_row(vres_bias[i - 1]))
        r, lw, k, v, a, b, gate, qmem = _a_pre(x, v_first if i > 0 else None, w, tm)
        if i == 0:
            v_first = v
        y_mix = _scan(r, lw, k, v, a, b, gate, _row(a_lnx_g[i]), _row(a_lnx_b[i]),
                      _row(a_r_k[i]), tc)
        wp = dict(w_out=a_w_out[i].astype(BF16), norm2_g=_row(a_norm2_g[i]),
                  ffn_gu=a_ffn_gu[i].astype(BF16), ffn_down=a_ffn_down[i].astype(BF16))
        x = _post(x, y_mix, qmem, memkv, i, wp, None, tm)

    lat = kv_latent_g.shape[0]
    heads = b_q_up.shape[2] // (QK_NOPE + QK_ROPE)
    per_kv = kv_w_up.shape[1] // heads
    wk_blocks, wv_blocks = [], []
    for hd in range(heads):
        blk = kv_w_up[:, hd * per_kv:(hd + 1) * per_kv]
        wk_blocks.append(jnp.pad(blk[:, :QK_NOPE], ((0, 0), (0, MLA_PAD - QK_NOPE))))
        wv_blocks.append(blk[:, QK_NOPE:])
    wkv = dict(kv_norm_g=_row(kv_norm_g),
               kv_wd=jnp.concatenate([kv_w_down[:, :lat], _rope_cols(kv_w_down[:, lat:], False),
                                      _rope_cols(kv_w_down[:, lat:], True)], axis=1).astype(BF16),
               kv_latent_g=_row(kv_latent_g),
               kv_wk=jnp.concatenate(wk_blocks, axis=1).astype(BF16),
               kv_wv=jnp.concatenate(wv_blocks, axis=1).astype(BF16))
    k_all, v_all = _kv(x, cos_t, sin_t, wkv, tm)

    for j in range(n_b):
        wq = dict(norm_g=_row(b_norm1_g[j]), w_in=b_w_in[j].astype(BF16),
                  q_norm_g=_row(b_q_norm_g[j]),
                  q_up_a=_q_up_padded(b_q_up[j], heads, False),
                  q_up_b=_q_up_padded(b_q_up[j], heads, True))
        q, qmem = _b_pre(x, cos_t, sin_t, wq, tm)
        o = _flash(q, k_all, v_all, tq)
        wp = dict(w_out=b_w_out[j].astype(BF16), norm2_g=_row(b_norm2_g[j]),
                  ffn_gu=b_ffn_gu[j].astype(BF16), ffn_down=b_ffn_down[j].astype(BF16))
        x = _post(x, o, qmem, memkv, n_a + j, wp,
                  _row(final_norm_g) if j == n_b - 1 else None, tm)
    return x
```

```python
import functools

import numpy as np
import jax
import jax.numpy as jnp
from jax import lax
from jax.experimental import pallas as pl
from jax.experimental.pallas import tpu as pltpu

F32 = jnp.float32
BF16 = jnp.bfloat16

HEAD = 64
CHUNK = 64
GROUP = 4
GL = GROUP * HEAD
NORM_EPS = 1e-6
LNX_EPS = 64e-5
QK_NOPE = 64
QK_ROPE = 32
MLA_PAD = 128
VMEM_LIMIT = 56 * 1024 * 1024
TILES = dict(tm=512, tc=512, tq=1024)


def _dot(a, b):
    return jnp.dot(a.astype(BF16), b.astype(BF16), preferred_element_type=F32)


def _dot_nt(a, b):
    return lax.dot_general(a.astype(BF16), b.astype(BF16), (((1,), (1,)), ((), ())),
                           preferred_element_type=F32)


def _dot_tn(a, b):
    return lax.dot_general(a.astype(BF16), b.astype(BF16), (((0,), (0,)), ((), ())),
                           preferred_element_type=F32)


def _split(x, terms):
    out = []
    for _ in range(terms):
        p = x.astype(BF16)
        out.append(p)
        x = x - p.astype(F32)
    return out


def _dot_ones_lhs(ones_bf16, x, terms=3):
    acc = None
    for p in _split(x, terms):
        d = jnp.dot(ones_bf16, p, preferred_element_type=F32)
        acc = d if acc is None else acc + d
    return acc


def _rms(x, g, eps=NORM_EPS):
    ms = jnp.mean(x * x, axis=-1, keepdims=True)
    return x * lax.rsqrt(ms + eps) * g


def _sigmoid(x):
    return 1.0 / (1.0 + jnp.exp(-x))


def _shift_rows(x, carry_ref):
    tm = x.shape[0]
    first = lax.broadcasted_iota(jnp.int32, (tm, 1), 0) == 0
    prev = jnp.where(first, carry_ref[0:1, :], pltpu.roll(x, 1, 0))
    carry_ref[0:1, :] = x[tm - 1:tm, :]
    return prev


def _const_spec(shape):
    nd = len(shape)
    return pl.BlockSpec(shape, lambda *_: (0,) * nd, pipeline_mode=pl.Buffered(1))


def _params(sem):
    return pltpu.CompilerParams(dimension_semantics=sem, vmem_limit_bytes=VMEM_LIMIT)


def _tile(n, pref):
    t = min(n, pref)
    assert n % t == 0, (n, t)
    return t


def _memkv_kernel(mem_ref, g_ref, w_ref, o_ref):
    mn = _rms(mem_ref[0], g_ref[...])
    o_ref[0] = _dot(mn, w_ref[...]).astype(o_ref.dtype)


def _memkv(mem, g, w_all):
    bsz, n_mem, d = mem.shape
    n = w_all.shape[1]
    return pl.pallas_call(
        _memkv_kernel,
        grid=(bsz,),
        in_specs=[pl.BlockSpec((1, n_mem, d), lambda b: (b, 0, 0)),
                  _const_spec((1, d)), _const_spec((d, n))],
        out_specs=pl.BlockSpec((1, n_mem, n), lambda b: (b, 0, 0)),
        out_shape=jax.ShapeDtypeStruct((bsz, n_mem, n), BF16),
        compiler_params=_params(("parallel",)),
        name="memkv",
    )(mem, g, w_all)


def _a_pre_kernel(has_vres, rdim, rwkv_in, lora, *refs):
    it = iter(refs)
    x_ref = next(it)
    vfirst_ref = next(it) if has_vres else None
    g_ref, w_in_ref, mu_ref = next(it), next(it), next(it)
    dup_ref, dbias_ref, aup_ref, abias_ref, gup_ref = (next(it) for _ in range(5))
    kk_ref, ka_ref, bd_ref = next(it), next(it), next(it)
    if has_vres:
        vmu_ref, vdown_ref, vup_ref, vbias_ref = (next(it) for _ in range(4))
    r_ref, lw_ref, k_ref, v_ref, a_ref, b_ref, gate_ref, qmem_ref = (next(it) for _ in range(8))
    carry_p = next(it)
    carry_h = next(it) if has_vres else None

    @pl.when(pl.program_id(1) == 0)
    def _():
        carry_p[...] = jnp.zeros_like(carry_p)
        if has_vres:
            carry_h[...] = jnp.zeros_like(carry_h)

    h = _rms(x_ref[0], g_ref[...])
    proj = _dot(h, w_in_ref[...])
    qmem_ref[0] = proj[:, rwkv_in:]
    p = proj[:, :rwkv_in]
    pm = p + (_shift_rows(p, carry_p) - p) * mu_ref[...]

    r = pm[:, 0:rdim]
    k = pm[:, rdim:2 * rdim]
    v = pm[:, 2 * rdim:3 * rdim]
    lo = pm[:, 3 * rdim:3 * rdim + lora]
    g_lo = pm[:, 3 * rdim + lora:]

    z = dbias_ref[...] + _dot(jnp.tanh(lo), dup_ref[...])
    nz = -z
    softplus = jnp.maximum(nz, 0.0) + jnp.log1p(jnp.exp(-jnp.abs(nz)))
    log_w = -softplus - 0.5
    lw_ref[0] = -jnp.exp(log_w)
    lr = _sigmoid(abias_ref[...] + _dot(lo, aup_ref[...]))
    gate_ref[0] = _dot(_sigmoid(g_lo), gup_ref[...])

    if has_vres:
        hv = h + (_shift_rows(h, carry_h) - h) * vmu_ref[...]
        t = _dot(hv, vdown_ref[...])
        sg = _sigmoid(vbias_ref[...] + _dot(t, vup_ref[...]))
        v = v + (vfirst_ref[0] - v) * sg

    kkr = k * kk_ref[...]
    norm = jnp.sqrt(_dot(kkr * kkr, bd_ref[...]))
    kk = kkr / jnp.maximum(norm, 1e-12)
    r_ref[0] = r
    k_ref[0] = k * (1.0 + (lr - 1.0) * ka_ref[...])
    v_ref[0] = v
    a_ref[0] = -kk
    b_ref[0] = kk * lr


def _a_pre(x, vfirst, w, tm):
    bsz, s_len, d = x.shape
    rdim = w["k_k"].shape[1]
    n_in = w["w_in"].shape[1]
    rwkv_in = w["mu"].shape[1]
    mem_dim = n_in - rwkv_in
    has_vres = vfirst is not None
    tok = lambda n: pl.BlockSpec((1, tm, n), lambda b, j: (b, j, 0))
    args, specs = [x], [tok(d)]
    if has_vres:
        args.append(vfirst)
        specs.append(tok(rdim))
    names = ["norm_g", "w_in", "mu", "decay_up", "decay_bias", "aaa_up", "aaa_bias", "gate_up",
             "k_k", "k_a", "bd768"]
    if has_vres:
        names += ["vres_mu", "vres_down", "vres_up", "vres_bias"]
    for n in names:
        args.append(w[n])
        specs.append(_const_spec(w[n].shape))
    scratch = [pltpu.VMEM((8, rwkv_in), F32)]
    if has_vres:
        scratch.append(pltpu.VMEM((8, d), F32))
    out_shape = [jax.ShapeDtypeStruct((bsz, s_len, rdim), F32)] * 7 + \
                [jax.ShapeDtypeStruct((bsz, s_len, mem_dim), F32)]
    out_specs = [tok(rdim)] * 7 + [tok(mem_dim)]
    return pl.pallas_call(
        functools.partial(_a_pre_kernel, has_vres, rdim, rwkv_in, w["decay_up"].shape[0]),
        grid=(bsz, s_len // tm),
        in_specs=specs, out_specs=out_specs, out_shape=out_shape,
        scratch_shapes=scratch,
        compiler_params=_params(("parallel", "arbitrary")),
        name="a_pre_vres" if has_vres else "a_pre",
    )(*args)


def _stack_heads(x_bf16, hm_ref):
    return jnp.concatenate([x_bf16 * hm_ref[h] for h in range(GROUP)], axis=0)


def _scan_prepare(ins, consts):
    tri, hm_ref, mask2, icat, bd = consts
    c = CHUNK
    gc = GROUP * c
    each = range(len(ins))
    bd_b = bd[...]
    r, lw, k, a, b = zip(*ins)

    cl = [_dot_ones_lhs(tri[...], lw[g]) for g in each]
    w_end = [jnp.exp(cl[g][c - 1:c, :]) for g in each]
    e_inc = [jnp.exp(cl[g]) for g in each]
    e_exc = [jnp.exp(cl[g] - lw[g]) for g in each]
    e_inv = [jnp.exp(-cl[g]) for g in each]
    kt = [k[g] * e_inv[g] for g in each]
    bt = [b[g] * e_inv[g] for g in each]
    ar = [jnp.concatenate([a[g] * e_exc[g], r[g] * e_inc[g]], axis=0).astype(BF16) for g in each]
    bks = [jnp.concatenate([_stack_heads(bt[g].astype(BF16), hm_ref),
                            _stack_heads(kt[g].astype(BF16), hm_ref)], axis=0) for g in each]
    m = [_dot_nt(ar[g], bks[g]) * mask2[...] for g in each]
    l_ab = [m[g][:c, :gc] for g in each]
    bkh = [jnp.concatenate([bt[g] * w_end[g], kt[g] * w_end[g]], axis=0).astype(BF16) for g in each]

    def blockdiag(p_bf16):
        return jnp.concatenate([p_bf16] * GROUP, axis=0) * bd_b

    pw = [l_ab[g].astype(BF16) for g in each]
    pw = [jnp.dot(pw[g], blockdiag(pw[g]), preferred_element_type=F32) for g in each]
    inv = [icat[...] + l_ab[g] for g in each]
    steps = int(np.log2(c)) - 1
    for it in range(steps):
        pbd = [blockdiag(pw[g].astype(BF16)) for g in each]
        if it + 1 < steps:
            both = [jnp.dot(jnp.concatenate([inv[g], pw[g]], axis=0).astype(BF16), pbd[g],
                            preferred_element_type=F32) for g in each]
            inv = [inv[g] + both[g][:c] for g in each]
            pw = [both[g][c:] for g in each]
        else:
            inv = [inv[g] + jnp.dot(inv[g].astype(BF16), pbd[g], preferred_element_type=F32)
                   for g in each]
    return (ar, [m[g][:c, gc:].astype(BF16) for g in each], [m[g][c:, :].astype(BF16) for g in each],
            [inv[g].astype(BF16) for g in each], bkh, w_end)


def _scan_advance(pre, vb, states, hm_ref, bd_f):
    c = CHUNK
    each = range(len(states))
    ar, l_ak, m_r, inv, bkh, w_end = pre
    ars = [_dot_nt(ar[g], states[g]) for g in each]
    vs = [_stack_heads(vb[g], hm_ref) for g in each]
    z = [ars[g][:c] + jnp.dot(l_ak[g], vs[g], preferred_element_type=F32) for g in each]
    u = [jnp.dot(inv[g], _stack_heads(z[g].astype(BF16), hm_ref), preferred_element_type=F32)
         for g in each]
    ub = [u[g].astype(BF16) for g in each]
    y = [ars[g][c:] + jnp.dot(m_r[g], jnp.concatenate([_stack_heads(ub[g], hm_ref), vs[g]], axis=0),
                              preferred_element_type=F32) for g in each]
    s_new = [states[g] * w_end[g]
             + bd_f * _dot_tn(jnp.concatenate([ub[g], vb[g]], axis=0), bkh[g]) for g in each]
    return y, s_new


def _scan_kernel(n_groups, n_chunks, per_iter, r_ref, lw_ref, k_ref, v_ref, a_ref, b_ref, gate_ref,
                 lnxg_ref, lnxb_ref, rk_ref, tri_ref, hm_ref, mask2_ref, icat_ref, bd_ref,
                 o_ref, state_ref, ar_scr, lak_scr, mr_scr, inv_scr, bkh_scr, wend_scr, y_scr):
    @pl.when(pl.program_id(1) == 0)
    def _():
        state_ref[...] = jnp.zeros_like(state_ref)

    c = CHUNK
    gc = GROUP * c
    consts = (tri_ref, hm_ref, mask2_ref, icat_ref, bd_ref)
    seqs = [(bi, g) for bi in range(r_ref.shape[0]) for g in range(n_groups)]
    lanes = [slice(g * GL, (g + 1) * GL) for g in range(n_groups)]
    cat = [slice(g * gc, (g + 1) * gc) for g in range(n_groups)]
    cat2 = [slice(g * 2 * gc, (g + 1) * 2 * gc) for g in range(n_groups)]

    def rows(ci, n):
        return pl.ds(pl.multiple_of(ci * n, n), n)

    def prepare(pi, carry):
        where = [(pi * per_iter + cc, bi, g) for cc in range(per_iter) for bi, g in seqs]
        ins = [tuple(ref[bi, rows(ci, c), lanes[g]] for ref in (r_ref, lw_ref, k_ref, a_ref, b_ref))
               for ci, bi, g in where]
        ar, l_ak, m_r, inv, bkh, w_end = _scan_prepare(ins, consts)
        for n, (ci, bi, g) in enumerate(where):
            ar_scr[bi, rows(ci, 2 * c), lanes[g]] = ar[n]
            lak_scr[bi, rows(ci, c), cat[g]] = l_ak[n]
            mr_scr[bi, rows(ci, c), cat2[g]] = m_r[n]
            inv_scr[bi, rows(ci, c), cat[g]] = inv[n]
            bkh_scr[bi, rows(ci, 2 * c), lanes[g]] = bkh[n]
            wend_scr[bi, ci, :, lanes[g]] = w_end[n]
        return carry

    lax.fori_loop(0, n_chunks // per_iter, prepare, 0)

    bd_f = bd_ref[...].astype(F32)

    def advance(ci, carry):
        pre = ([ar_scr[bi, rows(ci, 2 * c), lanes[g]] for bi, g in seqs],
               [lak_scr[bi, rows(ci, c), cat[g]] for bi, g in seqs],
               [mr_scr[bi, rows(ci, c), cat2[g]] for bi, g in seqs],
               [inv_scr[bi, rows(ci, c), cat[g]] for bi, g in seqs],
               [bkh_scr[bi, rows(ci, 2 * c), lanes[g]] for bi, g in seqs],
               [wend_scr[bi, ci, :, lanes[g]] for bi, g in seqs])
        vb = [v_ref[bi, rows(ci, c), lanes[g]].astype(BF16) for bi, g in seqs]
        y, s_new = _scan_advance(pre, vb, [state_ref[n] for n in range(len(seqs))], hm_ref, bd_f)
        for n, (bi, g) in enumerate(seqs):
            state_ref[n] = s_new[n]
            y_scr[bi, rows(ci, c), lanes[g]] = y[n]
        return carry

    lax.fori_loop(0, n_chunks, advance, 0)

    bd_b = bd_ref[...]
    bb, tc = y_scr.shape[0], y_scr.shape[1]
    flat = lambda t: t.reshape(bb * tc, GL)
    for ln in lanes:
        y = flat(y_scr[:, :, ln])
        mean = _dot(y, bd_b) * (1.0 / HEAD)
        d = y - mean
        var = _dot(d * d, bd_b) * (1.0 / HEAD)
        v = flat(v_ref[:, :, ln])
        bonus = _dot(flat(r_ref[:, :, ln]) * flat(k_ref[:, :, ln]) * rk_ref[:, ln], bd_b) * v
        gn = d * lax.rsqrt(var + LNX_EPS) * lnxg_ref[:, ln] + lnxb_ref[:, ln]
        out = (gn + bonus) * flat(gate_ref[:, :, ln])
        o_ref[:, :, ln] = out.reshape(bb, tc, GL).astype(o_ref.dtype)


def _scan_consts():
    c, g = CHUNK, GROUP
    gc = g * c
    i = np.arange(c)[:, None]
    j = np.arange(gc)[None, :] % c
    strict = (j < i).astype(np.float32)
    incl = (j <= i).astype(np.float32)
    mask2 = np.concatenate([np.concatenate([strict, strict], 1),
                            np.concatenate([incl, incl], 1)], 0)
    icat = (j == i).astype(np.float32)
    tri = (np.arange(c)[None, :] <= np.arange(c)[:, None]).astype(np.float32)
    lane_head = np.arange(GL) // HEAD
    hm = (lane_head[None, None, :] == np.arange(g)[:, None, None]).astype(np.float32)
    bd = (lane_head[:, None] == lane_head[None, :]).astype(np.float32)
    return (jnp.asarray(tri, BF16), jnp.asarray(hm, BF16), jnp.asarray(mask2, F32),
            jnp.asarray(icat, F32), jnp.asarray(bd, BF16))


def _scan(r, lw, k, v, a, b, gate, lnx_g, lnx_b, r_k, rows_per_step):
    bsz, s_len, rdim = r.shape
    n_groups = rdim // GL
    bb = next(n for n in (4, 2, 1) if bsz % n == 0)
    tc = max(CHUNK, rows_per_step // bb)
    assert s_len % tc == 0
    n_chunks = tc // CHUNK
    per_iter = max(1, min(n_chunks, 12 // (bb * n_groups)))
    assert n_chunks % per_iter == 0
    gc = GROUP * CHUNK
    tok = pl.BlockSpec((bb, tc, rdim), lambda bi, j: (bi, j, 0))
    consts = _scan_consts()
    small = [lnx_g, lnx_b, r_k] + list(consts)
    return pl.pallas_call(
        functools.partial(_scan_kernel, n_groups, n_chunks, per_iter),
        grid=(bsz // bb, s_len // tc),
        in_specs=[tok] * 7 + [_const_spec(t.shape) for t in small],
        out_specs=tok,
        out_shape=jax.ShapeDtypeStruct((bsz, s_len, rdim), BF16),
        scratch_shapes=[pltpu.VMEM((bb * n_groups, GL, GL), F32),
                        pltpu.VMEM((bb, 2 * tc, rdim), BF16),
                        pltpu.VMEM((bb, tc, n_groups * gc), BF16),
                        pltpu.VMEM((bb, tc, n_groups * 2 * gc), BF16),
                        pltpu.VMEM((bb, tc, n_groups * gc), BF16),
                        pltpu.VMEM((bb, 2 * tc, rdim), BF16),
                        pltpu.VMEM((bb, n_chunks, 1, rdim), F32),
                        pltpu.VMEM((bb, tc, rdim), F32)],
        compiler_params=_params(("parallel", "arbitrary")),
        name="rwkv_scan",
    )(r, lw, k, v, a, b, gate, *small)


def _post_kernel(mem_heads, ffn_tiles, final, x_ref, mix_ref, qmem_ref, mk_ref, mv_ref, hm_ref,
                 w_out_ref, g2_ref, w_gu_ref, w_dn_ref, gf_ref, o_ref):
    mix_dim = mix_ref.shape[2]
    fh = w_dn_ref.shape[0]
    th = fh // ffn_tiles

    qm = (qmem_ref[0] * (HEAD ** -0.5)).astype(BF16)
    mk = mk_ref[0]
    mv = mv_ref[0]
    m = None
    for hd in range(mem_heads):
        s = _dot_nt(qm * hm_ref[hd], mk)
        s = s - jnp.max(s, axis=-1, keepdims=True)
        e = jnp.exp(s)
        p = e / jnp.sum(e, axis=-1, keepdims=True)
        o = _dot(p, mv) * hm_ref[hd].astype(F32)
        m = o if m is None else m + o

    x1 = x_ref[0] + _dot(mix_ref[0], w_out_ref[0:mix_dim, :]) + _dot(m, w_out_ref[mix_dim:, :])
    h2 = _rms(x1, g2_ref[...]).astype(BF16)
    acc = None
    for t in range(ffn_tiles):
        gt = jnp.dot(h2, w_gu_ref[:, t * th:(t + 1) * th], preferred_element_type=F32)
        ut = jnp.dot(h2, w_gu_ref[:, fh + t * th:fh + (t + 1) * th], preferred_element_type=F32)
        act = gt * _sigmoid(gt) * ut
        dt = _dot(act, w_dn_ref[t * th:(t + 1) * th, :])
        acc = dt if acc is None else acc + dt
    x2 = x1 + acc
    if final:
        x2 = _rms(x2, gf_ref[...])
    o_ref[0] = x2


def _post(x, mix, qmem, memkv, layer, w, final_g, tm):
    bsz, s_len, d = x.shape
    mix_dim = mix.shape[2]
    mem_dim = qmem.shape[2]
    n_mem = memkv.shape[1]
    mem_heads = mem_dim // HEAD
    fh = w["ffn_down"].shape[0]
    ffn_tiles = 2 if fh % 256 == 0 else 1
    lane_head = np.arange(mem_dim) // HEAD
    hm = jnp.asarray((lane_head[None, None, :] == np.arange(mem_heads)[:, None, None]), BF16)
    tok = lambda n: pl.BlockSpec((1, tm, n), lambda b, j: (b, j, 0))
    final = final_g is not None
    gf = final_g if final else w["norm2_g"]
    return pl.pallas_call(
        functools.partial(_post_kernel, mem_heads, ffn_tiles, final),
        grid=(bsz, s_len // tm),
        in_specs=[tok(d), tok(mix_dim), tok(mem_dim),
                  pl.BlockSpec((1, n_mem, mem_dim), lambda b, j: (b, 0, 2 * layer)),
                  pl.BlockSpec((1, n_mem, mem_dim), lambda b, j: (b, 0, 2 * layer + 1)),
                  _const_spec(hm.shape), _const_spec(w["w_out"].shape),
                  _const_spec(w["norm2_g"].shape), _const_spec(w["ffn_gu"].shape),
                  _const_spec(w["ffn_down"].shape), _const_spec(gf.shape)],
        out_specs=tok(d),
        out_shape=jax.ShapeDtypeStruct((bsz, s_len, d), F32),
        compiler_params=_params(("parallel", "parallel")),
        name="post_final" if final else "post",
    )(x, mix, qmem, memkv, memkv, hm, w["w_out"], w["norm2_g"], w["ffn_gu"], w["ffn_down"], gf)


def _kv_kernel(lat, x_ref, cos_ref, sin_ref, g_ref, wd_ref, lg_ref, wk_ref, wv_ref, k_ref, v_ref):
    hk = _rms(x_ref[0], g_ref[...])
    ckr = _dot(hk, wd_ref[...])
    ckv = _rms(ckr[:, :lat], lg_ref[...])
    krot = ckr[:, lat:lat + MLA_PAD] * cos_ref[0] + ckr[:, lat + MLA_PAD:] * sin_ref[0]
    kn = _dot(ckv, wk_ref[...])
    heads = kn.shape[1] // MLA_PAD
    k_ref[0] = (kn + jnp.concatenate([krot] * heads, axis=1)).astype(k_ref.dtype)
    v_ref[0] = _dot(ckv, wv_ref[...]).astype(v_ref.dtype)


def _kv(x, cos_t, sin_t, w, tm):
    bsz, s_len, d = x.shape
    lat = w["kv_latent_g"].shape[1]
    nk = w["kv_wk"].shape[1]
    nv = w["kv_wv"].shape[1]
    tok = lambda n: pl.BlockSpec((1, tm, n), lambda b, j: (b, j, 0))
    names = ["kv_norm_g", "kv_wd", "kv_latent_g", "kv_wk", "kv_wv"]
    return pl.pallas_call(
        functools.partial(_kv_kernel, lat),
        grid=(bsz, s_len // tm),
        in_specs=[tok(d), tok(MLA_PAD), tok(MLA_PAD)] + [_const_spec(w[n].shape) for n in names],
        out_specs=[tok(nk), tok(nv)],
        out_shape=[jax.ShapeDtypeStruct((bsz, s_len, nk), BF16),
                   jax.ShapeDtypeStruct((bsz, s_len, nv), BF16)],
        compiler_params=_params(("parallel", "parallel")),
        name="mla_kv",
    )(x, cos_t, sin_t, *[w[n] for n in names])


def _b_pre_kernel(q_lora, scale, x_ref, cos_ref, sin_ref, g_ref, w_in_ref, qg_ref, qa_ref, qb_ref,
                  q_ref, qmem_ref):
    h = _rms(x_ref[0], g_ref[...])
    proj = _dot(h, w_in_ref[...])
    qmem_ref[0] = proj[:, q_lora:]
    cq = _rms(proj[:, :q_lora], qg_ref[...]).astype(BF16)
    qa = jnp.dot(cq, qa_ref[...], preferred_element_type=F32)
    qb = jnp.dot(cq, qb_ref[...], preferred_element_type=F32)
    heads = qa.shape[1] // MLA_PAD
    cos_t = jnp.concatenate([cos_ref[0]] * heads, axis=1)
    sin_t = jnp.concatenate([sin_ref[0]] * heads, axis=1)
    q_ref[0] = ((qa * cos_t + qb * sin_t) * scale).astype(q_ref.dtype)


def _b_pre(x, cos_t, sin_t, w, tm):
    bsz, s_len, d = x.shape
    q_lora = w["q_norm_g"].shape[1]
    mem_dim = w["w_in"].shape[1] - q_lora
    nq = w["q_up_a"].shape[1]
    scale = (QK_NOPE + QK_ROPE) ** -0.5 * float(np.log2(np.e))
    tok = lambda n: pl.BlockSpec((1, tm, n), lambda b, j: (b, j, 0))
    names = ["norm_g", "w_in", "q_norm_g", "q_up_a", "q_up_b"]
    return pl.pallas_call(
        functools.partial(_b_pre_kernel, q_lora, scale),
        grid=(bsz, s_len // tm),
        in_specs=[tok(d), tok(MLA_PAD), tok(MLA_PAD)] + [_const_spec(w[n].shape) for n in names],
        out_specs=[tok(nq), tok(mem_dim)],
        out_shape=[jax.ShapeDtypeStruct((bsz, s_len, nq), BF16),
                   jax.ShapeDtypeStruct((bsz, s_len, mem_dim), F32)],
        compiler_params=_params(("parallel", "parallel")),
        name="mla_q",
    )(x, cos_t, sin_t, *[w[n] for n in names])


def _flash_kernel(t, q_ref, k_ref, v_ref, o_ref, m_ref, l_ref, acc_ref, sa_ref, sb_ref):
    i = pl.program_id(2)
    reps = t // 128
    m_ref[...] = jnp.full_like(m_ref, -jnp.inf)
    l_ref[...] = jnp.zeros_like(l_ref)
    acc_ref[...] = jnp.zeros_like(acc_ref)
    low = lax.broadcasted_iota(jnp.int32, (1, 2 * HEAD), 1) < HEAD

    def scores(j, s_ref):
        rows = pl.ds(pl.multiple_of(j * t, t), t)
        for hd in range(2):
            q = q_ref[0, :, hd * MLA_PAD:(hd + 1) * MLA_PAD]
            k = k_ref[0, rows, hd * MLA_PAD:(hd + 1) * MLA_PAD]
            s_ref[hd] = lax.dot_general(q, k, (((1,), (1,)), ((), ())),
                                        preferred_element_type=F32)

    def apply(j, s_ref, diagonal):
        rows = pl.ds(pl.multiple_of(j * t, t), t)
        vpair = v_ref[0, rows, :]
        vhead = (jnp.where(low, vpair, 0), jnp.where(low, 0, vpair))
        if diagonal:
            visible = (lax.broadcasted_iota(jnp.int32, (t, t), 0)
                       >= lax.broadcasted_iota(jnp.int32, (t, t), 1))
        alphas, pvs = [], []
        for hd in range(2):
            s = s_ref[hd]
            if diagonal:
                s = jnp.where(visible, s, -jnp.inf)
            m_prev = m_ref[hd]
            m_new = jnp.maximum(m_prev, jnp.max(s, axis=-1, keepdims=True))
            alpha = jnp.exp2(m_prev - m_new)
            p = jnp.exp2(s - jnp.concatenate([m_new] * reps, axis=1))
            part = p[:, 0:128]
            for c in range(1, reps):
                part = part + p[:, c * 128:(c + 1) * 128]
            l_ref[hd] = alpha * l_ref[hd] + part
            m_ref[hd] = m_new
            pvs.append(jnp.dot(p.astype(BF16), vhead[hd], preferred_element_type=F32))
            alphas.append(alpha)
        acc_ref[...] = acc_ref[...] * jnp.where(low, alphas[0], alphas[1]) + pvs[0] + pvs[1]

    scores(0, sa_ref)

    def two_tiles(n, carry):
        scores(2 * n + 1, sb_ref)
        apply(2 * n, sa_ref, False)
        scores(2 * n + 2, sa_ref)
        apply(2 * n + 1, sb_ref, False)
        return carry

    lax.fori_loop(0, i // 2, two_tiles, 0)

    @pl.when(i % 2 == 1)
    def _():
        scores(i, sb_ref)
        apply(i - 1, sa_ref, False)
        apply(i, sb_ref, True)

    @pl.when(i % 2 == 0)
    def _():
        apply(i, sa_ref, True)

    l0 = jnp.sum(l_ref[0], axis=-1, keepdims=True)
    l1 = jnp.sum(l_ref[1], axis=-1, keepdims=True)
    o_ref[0] = (acc_ref[...] / jnp.where(low, l0, l1)).astype(o_ref.dtype)


def _flash(q, k, v, t):
    bsz, s_len, nq = q.shape
    pairs = nq // (2 * MLA_PAD)
    return pl.pallas_call(
        functools.partial(_flash_kernel, t),
        grid=(bsz, pairs, s_len // t),
        in_specs=[pl.BlockSpec((1, t, 2 * MLA_PAD), lambda b, p, i: (b, i, p)),
                  pl.BlockSpec((1, s_len, 2 * MLA_PAD), lambda b, p, i: (b, 0, p)),
                  pl.BlockSpec((1, s_len, 2 * HEAD), lambda b, p, i: (b, 0, p))],
        out_specs=pl.BlockSpec((1, t, 2 * HEAD), lambda b, p, i: (b, i, p)),
        out_shape=jax.ShapeDtypeStruct((bsz, s_len, pairs * 2 * HEAD), BF16),
        scratch_shapes=[pltpu.VMEM((2, t, 128), F32), pltpu.VMEM((2, t, 128), F32),
                        pltpu.VMEM((t, 2 * HEAD), F32),
                        pltpu.VMEM((2, t, t), F32), pltpu.VMEM((2, t, t), F32)],
        compiler_params=_params(("parallel", "parallel", "arbitrary")),
        name="mla_flash",
    )(q, k, v)


def _row(v):
    return v.reshape(1, -1).astype(F32)


def _pad_rows(w, total, offset):
    out = jnp.zeros((total, w.shape[1]), w.dtype)
    return out.at[offset:offset + w.shape[0]].set(w)


def _block_ones(n):
    h = np.arange(n) // HEAD
    return jnp.asarray(h[:, None] == h[None, :], BF16)


def _rope_cols(w_rope, swap):
    half = QK_ROPE // 2
    t1, t2 = w_rope[:, :half], w_rope[:, half:]
    if swap:
        t1, t2 = t2, t1
    z = lambda n: jnp.zeros((w_rope.shape[0], n), w_rope.dtype)
    return jnp.concatenate([z(QK_NOPE), t1, t2, z(MLA_PAD - QK_NOPE - QK_ROPE)], axis=1)


def _q_up_padded(q_up, heads, swap):
    per = QK_NOPE + QK_ROPE
    blocks = []
    for hd in range(heads):
        wh = q_up[:, hd * per:(hd + 1) * per]
        rope = _rope_cols(wh[:, QK_NOPE:], swap)
        if swap:
            blocks.append(rope)
        else:
            blocks.append(rope.at[:, :QK_NOPE].set(wh[:, :QK_NOPE]))
    return jnp.concatenate(blocks, axis=1).astype(BF16)


def kernel(x, mem, positions, mem_norm_g, a_norm1_g, a_w_in, a_shift_mu, a_decay_up, a_decay_bias, a_aaa_up, a_aaa_bias, a_gate_up, a_k_k, a_k_a, a_r_k, a_lnx_g, a_lnx_b, a_mem_kv, a_w_out, a_norm2_g, a_ffn_gu, a_ffn_down, vres_mu, vres_down, vres_up, vres_bias, kv_norm_g, kv_w_down, kv_latent_g, kv_w_up, b_norm1_g, b_w_in, b_q_norm_g, b_q_up, b_mem_kv, b_w_out, b_norm2_g, b_ffn_gu, b_ffn_down, final_norm_g):
    bsz, s_len, d = x.shape
    n_a = a_w_in.shape[0]
    n_b = b_w_in.shape[0]
    rdim = a_k_k.shape[1]
    assert CHUNK == HEAD and rdim % GL == 0 and s_len % CHUNK == 0
    tm = _tile(s_len, TILES["tm"])
    tc = _tile(s_len, TILES["tc"])
    tq = _tile(s_len, TILES["tq"])

    half = QK_ROPE // 2
    inv_freq = 10000.0 ** (-jnp.arange(half, dtype=F32) / half)
    ang = positions.astype(F32)[..., None] * inv_freq
    cos, sin = jnp.cos(ang), jnp.sin(ang)
    ones = jnp.ones((bsz, s_len, QK_NOPE), F32)
    zpad = jnp.zeros((bsz, s_len, MLA_PAD - QK_NOPE - QK_ROPE), F32)
    cos_t = jnp.concatenate([ones, cos, cos, zpad], axis=-1)
    sin_t = jnp.concatenate([0 * ones, -sin, sin, zpad], axis=-1)

    memkv = _memkv(mem, _row(mem_norm_g),
                   jnp.concatenate(list(a_mem_kv) + list(b_mem_kv), axis=1).astype(BF16))
    bd768 = _block_ones(rdim)

    lora = a_decay_up.shape[1] + a_aaa_up.shape[1]
    v_first = None
    for i in range(n_a):
        w = dict(norm_g=_row(a_norm1_g[i]), w_in=a_w_in[i].astype(BF16), mu=_row(a_shift_mu[i]),
                 decay_up=_pad_rows(a_decay_up[i], lora, 0).astype(BF16),
                 decay_bias=_row(a_decay_bias[i]),
                 aaa_up=_pad_rows(a_aaa_up[i], lora, a_decay_up.shape[1]).astype(BF16),
                 aaa_bias=_row(a_aaa_bias[i]), gate_up=a_gate_up[i].astype(BF16),
                 k_k=_row(a_k_k[i]), k_a=_row(a_k_a[i]), bd768=bd768)
        if i > 0:
            w.update(vres_mu=_row(vres_mu[i - 1]),
                     vres_down=jnp.pad(vres_down[i - 1], ((0, 0), (0, 128 - vres_down.shape[2]))).astype(BF16),
                     vres_up=_pad_rows(vres_up[i - 1], 128, 0).astype(BF16),
                     vres_bias=_row(vres_bias[i - 1]))
        r, lw, k, v, a, b, gate, qmem = _a_pre(x, v_first if i > 0 else None, w, tm)
        if i == 0:
            v_first = v
        y_mix = _scan(r, lw, k, v, a, b, gate, _row(a_lnx_g[i]), _row(a_lnx_b[i]),
                      _row(a_r_k[i]), tc)
        wp = dict(w_out=a_w_out[i].astype(BF16), norm2_g=_row(a_norm2_g[i]),
                  ffn_gu=a_ffn_gu[i].astype(BF16), ffn_down=a_ffn_down[i].astype(BF16))
        x = _post(x, y_mix, qmem, memkv, i, wp, None, tm)

    lat = kv_latent_g.shape[0]
    heads = b_q_up.shape[2] // (QK_NOPE + QK_ROPE)
    per_kv = kv_w_up.shape[1] // heads
    wk_blocks, wv_blocks = [], []
    for hd in range(heads):
        blk = kv_w_up[:, hd * per_kv:(hd + 1) * per_kv]
        wk_blocks.append(jnp.pad(blk[:, :QK_NOPE], ((0, 0), (0, MLA_PAD - QK_NOPE))))
        wv_blocks.append(blk[:, QK_NOPE:])
    wkv = dict(kv_norm_g=_row(kv_norm_g),
               kv_wd=jnp.concatenate([kv_w_down[:, :lat], _rope_cols(kv_w_down[:, lat:], False),
                                      _rope_cols(kv_w_down[:, lat:], True)], axis=1).astype(BF16),
               kv_latent_g=_row(kv_latent_g),
               kv_wk=jnp.concatenate(wk_blocks, axis=1).astype(BF16),
               kv_wv=jnp.concatenate(wv_blocks, axis=1).astype(BF16))
    k_all, v_all = _kv(x, cos_t, sin_t, wkv, tm)

    for j in range(n_b):
        wq = dict(norm_g=_row(b_norm1_g[j]), w_in=b_w_in[j].astype(BF16),
                  q_norm_g=_row(b_q_norm_g[j]),
                  q_up_a=_q_up_padded(b_q_up[j], heads, False),
                  q_up_b=_q_up_padded(b_q_up[j], heads, True))
        q, qmem = _b_pre(x, cos_t, sin_t, wq, tm)
        o = _flash(q, k_all, v_all, tq)
        wp = dict(w_out=b_w_out[j].astype(BF16), norm2_g=_row(b_norm2_g[j]),
                  ffn_gu=b_ffn_gu[j].astype(BF16), ffn_down=b_ffn_down[j].astype(BF16))
        x = _post(x, o, qmem, memkv, n_a + j, wp,
                  _row(final_norm_g) if j == n_b - 1 else None, tm)
    return x
```

```python
import functools

import numpy as np
import jax
import jax.numpy as jnp
from jax import lax
from jax.experimental import pallas as pl
from jax.experimental.pallas import tpu as pltpu

F32 = jnp.float32
BF16 = jnp.bfloat16

HEAD = 64
CHUNK = 64
GROUP = 4
GL = GROUP * HEAD
NORM_EPS = 1e-6
LNX_EPS = 64e-5
QK_NOPE = 64
QK_ROPE = 32
MLA_PAD = 128
VMEM_LIMIT = 56 * 1024 * 1024
TILES = dict(tm=512, tc=512, tq=1024)


def _dot(a, b):
    return jnp.dot(a.astype(BF16), b.astype(BF16), preferred_element_type=F32)


def _dot_nt(a, b):
    return lax.dot_general(a.astype(BF16), b.astype(BF16), (((1,), (1,)), ((), ())),
                           preferred_element_type=F32)


def _dot_tn(a, b):
    return lax.dot_general(a.astype(BF16), b.astype(BF16), (((0,), (0,)), ((), ())),
                           preferred_element_type=F32)


def _split(x, terms):
    out = []
    for _ in range(terms):
        p = x.astype(BF16)
        out.append(p)
        x = x - p.astype(F32)
    return out


def _dot_ones_lhs(ones_bf16, x, terms=3):
    acc = None
    for p in _split(x, terms):
        d = jnp.dot(ones_bf16, p, preferred_element_type=F32)
        acc = d if acc is None else acc + d
    return acc


def _rms(x, g, eps=NORM_EPS):
    ms = jnp.mean(x * x, axis=-1, keepdims=True)
    return x * lax.rsqrt(ms + eps) * g


def _sigmoid(x):
    return 1.0 / (1.0 + jnp.exp(-x))


def _shift_rows(x, carry_ref):
    tm = x.shape[0]
    first = lax.broadcasted_iota(jnp.int32, (tm, 1), 0) == 0
    prev = jnp.where(first, carry_ref[0:1, :], pltpu.roll(x, 1, 0))
    carry_ref[0:1, :] = x[tm - 1:tm, :]
    return prev


def _const_spec(shape):
    nd = len(shape)
    return pl.BlockSpec(shape, lambda *_: (0,) * nd, pipeline_mode=pl.Buffered(1))


def _params(sem):
    return pltpu.CompilerParams(dimension_semantics=sem, vmem_limit_bytes=VMEM_LIMIT)


def _tile(n, pref):
    t = min(n, pref)
    assert n % t == 0, (n, t)
    return t


def _memkv_kernel(mem_ref, g_ref, w_ref, o_ref):
    mn = _rms(mem_ref[0], g_ref[...])
    o_ref[0] = _dot(mn, w_ref[...]).astype(o_ref.dtype)


def _memkv(mem, g, w_all):
    bsz, n_mem, d = mem.shape
    n = w_all.shape[1]
    return pl.pallas_call(
        _memkv_kernel,
        grid=(bsz,),
        in_specs=[pl.BlockSpec((1, n_mem, d), lambda b: (b, 0, 0)),
                  _const_spec((1, d)), _const_spec((d, n))],
        out_specs=pl.BlockSpec((1, n_mem, n), lambda b: (b, 0, 0)),
        out_shape=jax.ShapeDtypeStruct((bsz, n_mem, n), BF16),
        compiler_params=_params(("parallel",)),
        name="memkv",
    )(mem, g, w_all)


def _a_pre_kernel(has_vres, rdim, rwkv_in, lora, *refs):
    it = iter(refs)
    x_ref = next(it)
    vfirst_ref = next(it) if has_vres else None
    g_ref, w_in_ref, mu_ref = next(it), next(it), next(it)
    dup_ref, dbias_ref, aup_ref, abias_ref, gup_ref = (next(it) for _ in range(5))
    kk_ref, ka_ref, bd_ref = next(it), next(it), next(it)
    if has_vres:
        vmu_ref, vdown_ref, vup_ref, vbias_ref = (next(it) for _ in range(4))
    r_ref, lw_ref, k_ref, v_ref, a_ref, b_ref, gate_ref, qmem_ref = (next(it) for _ in range(8))
    carry_p = next(it)
    carry_h = next(it) if has_vres else None

    @pl.when(pl.program_id(1) == 0)
    def _():
        carry_p[...] = jnp.zeros_like(carry_p)
        if has_vres:
            carry_h[...] = jnp.zeros_like(carry_h)

    h = _rms(x_ref[0], g_ref[...])
    proj = _dot(h, w_in_ref[...])
    qmem_ref[0] = proj[:, rwkv_in:]
    p = proj[:, :rwkv_in]
    pm = p + (_shift_rows(p, carry_p) - p) * mu_ref[...]

    r = pm[:, 0:rdim]
    k = pm[:, rdim:2 * rdim]
    v = pm[:, 2 * rdim:3 * rdim]
    lo = pm[:, 3 * rdim:3 * rdim + lora]
    g_lo = pm[:, 3 * rdim + lora:]

    z = dbias_ref[...] + _dot(jnp.tanh(lo), dup_ref[...])
    nz = -z
    softplus = jnp.maximum(nz, 0.0) + jnp.log1p(jnp.exp(-jnp.abs(nz)))
    log_w = -softplus - 0.5
    lw_ref[0] = -jnp.exp(log_w)
    lr = _sigmoid(abias_ref[...] + _dot(lo, aup_ref[...]))
    gate_ref[0] = _dot(_sigmoid(g_lo), gup_ref[...])

    if has_vres:
        hv = h + (_shift_rows(h, carry_h) - h) * vmu_ref[...]
        t = _dot(hv, vdown_ref[...])
        sg = _sigmoid(vbias_ref[...] + _dot(t, vup_ref[...]))
        v = v + (vfirst_ref[0] - v) * sg

    kkr = k * kk_ref[...]
    norm = jnp.sqrt(_dot(kkr * kkr, bd_ref[...]))
    kk = kkr / jnp.maximum(norm, 1e-12)
    r_ref[0] = r
    k_ref[0] = k * (1.0 + (lr - 1.0) * ka_ref[...])
    v_ref[0] = v
    a_ref[0] = -kk
    b_ref[0] = kk * lr


def _a_pre(x, vfirst, w, tm):
    bsz, s_len, d = x.shape
    rdim = w["k_k"].shape[1]
    n_in = w["w_in"].shape[1]
    rwkv_in = w["mu"].shape[1]
    mem_dim = n_in - rwkv_in
    has_vres = vfirst is not None
    tok = lambda n: pl.BlockSpec((1, tm, n), lambda b, j: (b, j, 0))
    args, specs = [x], [tok(d)]
    if has_vres:
        args.append(vfirst)
        specs.append(tok(rdim))
    names = ["norm_g", "w_in", "mu", "decay_up", "decay_bias", "aaa_up", "aaa_bias", "gate_up",
             "k_k", "k_a", "bd768"]
    if has_vres:
        names += ["vres_mu", "vres_down", "vres_up", "vres_bias"]
    for n in names:
        args.append(w[n])
        specs.append(_const_spec(w[n].shape))
    scratch = [pltpu.VMEM((8, rwkv_in), F32)]
    if has_vres:
        scratch.append(pltpu.VMEM((8, d), F32))
    out_shape = [jax.ShapeDtypeStruct((bsz, s_len, rdim), F32)] * 7 + \
                [jax.ShapeDtypeStruct((bsz, s_len, mem_dim), F32)]
    out_specs = [tok(rdim)] * 7 + [tok(mem_dim)]
    return pl.pallas_call(
        functools.partial(_a_pre_kernel, has_vres, rdim, rwkv_in, w["decay_up"].shape[0]),
        grid=(bsz, s_len // tm),
        in_specs=specs, out_specs=out_specs, out_shape=out_shape,
        scratch_shapes=scratch,
        compiler_params=_params(("parallel", "arbitrary")),
        name="a_pre_vres" if has_vres else "a_pre",
    )(*args)


def _stack_heads(x_bf16, hm_ref):
    return jnp.concatenate([x_bf16 * hm_ref[h] for h in range(GROUP)], axis=0)


def _scan_prepare(ins, consts, done):
    tri, hm_ref, mask2, icat, bd = consts
    c = CHUNK
    gc = GROUP * c
    each = range(len(ins))
    bd_b = bd[...]
    r, lw, k, a, b = zip(*ins)

    cl = [_dot_ones_lhs(tri[...], lw[g]) for g in each]
    yield
    w_end = [jnp.exp(cl[g][c - 1:c, :]) for g in each]
    e_inc = [jnp.exp(cl[g]) for g in each]
    e_exc = [jnp.exp(cl[g] - lw[g]) for g in each]
    e_inv = [jnp.exp(-cl[g]) for g in each]
    kt = [k[g] * e_inv[g] for g in each]
    bt = [b[g] * e_inv[g] for g in each]
    ar = [jnp.concatenate([a[g] * e_exc[g], r[g] * e_inc[g]], axis=0).astype(BF16) for g in each]
    bks = [jnp.concatenate([_stack_heads(bt[g].astype(BF16), hm_ref),
                            _stack_heads(kt[g].astype(BF16), hm_ref)], axis=0) for g in each]
    m = [_dot_nt(ar[g], bks[g]) * mask2[...] for g in each]
    l_ab = [m[g][:c, :gc] for g in each]
    bkh = [jnp.concatenate([bt[g] * w_end[g], kt[g] * w_end[g]], axis=0).astype(BF16) for g in each]
    yield

    def blockdiag(p_bf16):
        return jnp.concatenate([p_bf16] * GROUP, axis=0) * bd_b

    pw = [l_ab[g].astype(BF16) for g in each]
    pw = [jnp.dot(pw[g], blockdiag(pw[g]), preferred_element_type=F32) for g in each]
    inv = [icat[...] + l_ab[g] for g in each]
    yield
    steps = int(np.log2(c)) - 1
    for it in range(steps):
        pbd = [blockdiag(pw[g].astype(BF16)) for g in each]
        if it + 1 < steps:
            both = [jnp.dot(jnp.concatenate([inv[g], pw[g]], axis=0).astype(BF16), pbd[g],
                            preferred_element_type=F32) for g in each]
            inv = [inv[g] + both[g][:c] for g in each]
            pw = [both[g][c:] for g in each]
        else:
            inv = [inv[g] + jnp.dot(inv[g].astype(BF16), pbd[g], preferred_element_type=F32)
                   for g in each]
        yield
    done(ar, [m[g][:c, gc:].astype(BF16) for g in each], [m[g][c:, :].astype(BF16) for g in each],
         [inv[g].astype(BF16) for g in each], bkh, w_end)


def _scan_advance(pre, vb, states, hm_ref, bd_f, done):
    c = CHUNK
    each = range(len(states))
    ar, l_ak, m_r, inv, bkh, w_end = pre
    ars = [_dot_nt(ar[g], states[g]) for g in each]
    vs = [_stack_heads(vb[g], hm_ref) for g in each]
    z = [ars[g][:c] + jnp.dot(l_ak[g], vs[g], preferred_element_type=F32) for g in each]
    yield
    u = [jnp.dot(inv[g], _stack_heads(z[g].astype(BF16), hm_ref), preferred_element_type=F32)
         for g in each]
    ub = [u[g].astype(BF16) for g in each]
    yield
    y = [ars[g][c:] + jnp.dot(m_r[g], jnp.concatenate([_stack_heads(ub[g], hm_ref), vs[g]], axis=0),
                              preferred_element_type=F32) for g in each]
    s_new = [states[g] * w_end[g]
             + bd_f * _dot_tn(jnp.concatenate([ub[g], vb[g]], axis=0), bkh[g]) for g in each]
    done(y, s_new)


def _weave(main, *others):
    live = [main, *others]
    while live:
        for g in list(live):
            if next(g, live) is live:
                live.remove(g)


def _scan_kernel(n_groups, n_chunks, r_ref, lw_ref, k_ref, v_ref, a_ref, b_ref, gate_ref,
                 lnxg_ref, lnxb_ref, rk_ref, tri_ref, hm_ref, mask2_ref, icat_ref, bd_ref,
                 o_ref, state_ref, ar_scr, lak_scr, mr_scr, inv_scr, bkh_scr, wend_scr, y_scr):
    @pl.when(pl.program_id(1) == 0)
    def _():
        state_ref[...] = jnp.zeros_like(state_ref)

    c = CHUNK
    gc = GROUP * c
    consts = (tri_ref, hm_ref, mask2_ref, icat_ref, bd_ref)
    seqs = [(bi, g) for bi in range(r_ref.shape[0]) for g in range(n_groups)]
    lanes = [slice(g * GL, (g + 1) * GL) for g in range(n_groups)]
    cat = [slice(g * gc, (g + 1) * gc) for g in range(n_groups)]
    cat2 = [slice(g * 2 * gc, (g + 1) * 2 * gc) for g in range(n_groups)]
    bd_b = bd_ref[...]
    bd_f = bd_b.astype(F32)

    def rows(ci, n=c):
        return slice(ci * n, (ci + 1) * n)

    def prepare(ci):
        ins = [tuple(ref[bi, rows(ci), lanes[g]] for ref in (r_ref, lw_ref, k_ref, a_ref, b_ref))
               for bi, g in seqs]

        def done(ar, l_ak, m_r, inv, bkh, w_end):
            for n, (bi, g) in enumerate(seqs):
                ar_scr[bi, rows(ci, 2 * c), lanes[g]] = ar[n]
                lak_scr[bi, rows(ci), cat[g]] = l_ak[n]
                mr_scr[bi, rows(ci), cat2[g]] = m_r[n]
                inv_scr[bi, rows(ci), cat[g]] = inv[n]
                bkh_scr[bi, rows(ci, 2 * c), lanes[g]] = bkh[n]
                wend_scr[bi, ci, :, lanes[g]] = w_end[n]

        return _scan_prepare(ins, consts, done)

    def advance(ci):
        pre = ([ar_scr[bi, rows(ci, 2 * c), lanes[g]] for bi, g in seqs],
               [lak_scr[bi, rows(ci), cat[g]] for bi, g in seqs],
               [mr_scr[bi, rows(ci), cat2[g]] for bi, g in seqs],
               [inv_scr[bi, rows(ci), cat[g]] for bi, g in seqs],
               [bkh_scr[bi, rows(ci, 2 * c), lanes[g]] for bi, g in seqs],
               [wend_scr[bi, ci, :, lanes[g]] for bi, g in seqs])
        vb = [v_ref[bi, rows(ci), lanes[g]].astype(BF16) for bi, g in seqs]

        def done(y, s_new):
            for n, (bi, g) in enumerate(seqs):
                state_ref[n] = s_new[n]
                y_scr[bi, rows(ci), lanes[g]] = y[n]

        return _scan_advance(pre, vb, [state_ref[n] for n in range(len(seqs))], hm_ref, bd_f, done)

    _weave(prepare(0))
    for ci in range(n_chunks):
        _weave(advance(ci), *([prepare(ci + 1)] if ci + 1 < n_chunks else []))

    bb, tc = y_scr.shape[0], y_scr.shape[1]
    flat = lambda t: t.reshape(bb * tc, GL)
    for ln in lanes:
        y = flat(y_scr[:, :, ln])
        mean = _dot(y, bd_b) * (1.0 / HEAD)
        d = y - mean
        var = _dot(d * d, bd_b) * (1.0 / HEAD)
        v = flat(v_ref[:, :, ln])
        bonus = _dot(flat(r_ref[:, :, ln]) * flat(k_ref[:, :, ln]) * rk_ref[:, ln], bd_b) * v
        gn = d * lax.rsqrt(var + LNX_EPS) * lnxg_ref[:, ln] + lnxb_ref[:, ln]
        out = (gn + bonus) * flat(gate_ref[:, :, ln])
        o_ref[:, :, ln] = out.reshape(bb, tc, GL).astype(o_ref.dtype)


def _scan_consts():
    c, g = CHUNK, GROUP
    gc = g * c
    i = np.arange(c)[:, None]
    j = np.arange(gc)[None, :] % c
    strict = (j < i).astype(np.float32)
    incl = (j <= i).astype(np.float32)
    mask2 = np.concatenate([np.concatenate([strict, strict], 1),
                            np.concatenate([incl, incl], 1)], 0)
    icat = (j == i).astype(np.float32)
    tri = (np.arange(c)[None, :] <= np.arange(c)[:, None]).astype(np.float32)
    lane_head = np.arange(GL) // HEAD
    hm = (lane_head[None, None, :] == np.arange(g)[:, None, None]).astype(np.float32)
    bd = (lane_head[:, None] == lane_head[None, :]).astype(np.float32)
    return (jnp.asarray(tri, BF16), jnp.asarray(hm, BF16), jnp.asarray(mask2, F32),
            jnp.asarray(icat, F32), jnp.asarray(bd, BF16))


def _scan(r, lw, k, v, a, b, gate, lnx_g, lnx_b, r_k, rows_per_step):
    bsz, s_len, rdim = r.shape
    n_groups = rdim // GL
    bb = next(n for n in (4, 2, 1) if bsz % n == 0)
    tc = max(CHUNK, rows_per_step // bb)
    assert s_len % tc == 0
    n_chunks = tc // CHUNK
    gc = GROUP * CHUNK
    tok = pl.BlockSpec((bb, tc, rdim), lambda bi, j: (bi, j, 0))
    consts = _scan_consts()
    small = [lnx_g, lnx_b, r_k] + list(consts)
    return pl.pallas_call(
        functools.partial(_scan_kernel, n_groups, n_chunks),
        grid=(bsz // bb, s_len // tc),
        in_specs=[tok] * 7 + [_const_spec(t.shape) for t in small],
        out_specs=tok,
        out_shape=jax.ShapeDtypeStruct((bsz, s_len, rdim), BF16),
        scratch_shapes=[pltpu.VMEM((bb * n_groups, GL, GL), F32),
                        pltpu.VMEM((bb, 2 * tc, rdim), BF16),
                        pltpu.VMEM((bb, tc, n_groups * gc), BF16),
                        pltpu.VMEM((bb, tc, n_groups * 2 * gc), BF16),
                        pltpu.VMEM((bb, tc, n_groups * gc), BF16),
                        pltpu.VMEM((bb, 2 * tc, rdim), BF16),
                        pltpu.VMEM((bb, n_chunks, 1, rdim), F32),
                        pltpu.VMEM((bb, tc, rdim), F32)],
        compiler_params=_params(("parallel", "arbitrary")),
        name="rwkv_scan",
    )(r, lw, k, v, a, b, gate, *small)


def _post_kernel(mem_heads, ffn_tiles, final, x_ref, mix_ref, qmem_ref, mk_ref, mv_ref, hm_ref,
                 w_out_ref, g2_ref, w_gu_ref, w_dn_ref, gf_ref, o_ref):
    mix_dim = mix_ref.shape[2]
    fh = w_dn_ref.shape[0]
    th = fh // ffn_tiles

    qm = (qmem_ref[0] * (HEAD ** -0.5)).astype(BF16)
    mk = mk_ref[0]
    mv = mv_ref[0]
    m = None
    for hd in range(mem_heads):
        s = _dot_nt(qm * hm_ref[hd], mk)
        s = s - jnp.max(s, axis=-1, keepdims=True)
        e = jnp.exp(s)
        p = e / jnp.sum(e, axis=-1, keepdims=True)
        o = _dot(p, mv) * hm_ref[hd].astype(F32)
        m = o if m is None else m + o

    x1 = x_ref[0] + _dot(mix_ref[0], w_out_ref[0:mix_dim, :]) + _dot(m, w_out_ref[mix_dim:, :])
    h2 = _rms(x1, g2_ref[...]).astype(BF16)
    acc = None
    for t in range(ffn_tiles):
        gt = jnp.dot(h2, w_gu_ref[:, t * th:(t + 1) * th], preferred_element_type=F32)
        ut = jnp.dot(h2, w_gu_ref[:, fh + t * th:fh + (t + 1) * th], preferred_element_type=F32)
        act = gt * _sigmoid(gt) * ut
        dt = _dot(act, w_dn_ref[t * th:(t + 1) * th, :])
        acc = dt if acc is None else acc + dt
    x2 = x1 + acc
    if final:
        x2 = _rms(x2, gf_ref[...])
    o_ref[0] = x2


def _post(x, mix, qmem, memkv, layer, w, final_g, tm):
    bsz, s_len, d = x.shape
    mix_dim = mix.shape[2]
    mem_dim = qmem.shape[2]
    n_mem = memkv.shape[1]
    mem_heads = mem_dim // HEAD
    fh = w["ffn_down"].shape[0]
    ffn_tiles = 2 if fh % 256 == 0 else 1
    lane_head = np.arange(mem_dim) // HEAD
    hm = jnp.asarray((lane_head[None, None, :] == np.arange(mem_heads)[:, None, None]), BF16)
    tok = lambda n: pl.BlockSpec((1, tm, n), lambda b, j: (b, j, 0))
    final = final_g is not None
    gf = final_g if final else w["norm2_g"]
    return pl.pallas_call(
        functools.partial(_post_kernel, mem_heads, ffn_tiles, final),
        grid=(bsz, s_len // tm),
        in_specs=[tok(d), tok(mix_dim), tok(mem_dim),
                  pl.BlockSpec((1, n_mem, mem_dim), lambda b, j: (b, 0, 2 * layer)),
                  pl.BlockSpec((1, n_mem, mem_dim), lambda b, j: (b, 0, 2 * layer + 1)),
                  _const_spec(hm.shape), _const_spec(w["w_out"].shape),
                  _const_spec(w["norm2_g"].shape), _const_spec(w["ffn_gu"].shape),
                  _const_spec(w["ffn_down"].shape), _const_spec(gf.shape)],
        out_specs=tok(d),
        out_shape=jax.ShapeDtypeStruct((bsz, s_len, d), F32),
        compiler_params=_params(("parallel", "parallel")),
        name="post_final" if final else "post",
    )(x, mix, qmem, memkv, memkv, hm, w["w_out"], w["norm2_g"], w["ffn_gu"], w["ffn_down"], gf)


def _kv_kernel(lat, x_ref, cos_ref, sin_ref, g_ref, wd_ref, lg_ref, wk_ref, wv_ref, k_ref, v_ref):
    hk = _rms(x_ref[0], g_ref[...])
    ckr = _dot(hk, wd_ref[...])
    ckv = _rms(ckr[:, :lat], lg_ref[...])
    krot = ckr[:, lat:lat + MLA_PAD] * cos_ref[0] + ckr[:, lat + MLA_PAD:] * sin_ref[0]
    kn = _dot(ckv, wk_ref[...])
    heads = kn.shape[1] // MLA_PAD
    k_ref[0] = (kn + jnp.concatenate([krot] * heads, axis=1)).astype(k_ref.dtype)
    v_ref[0] = _dot(ckv, wv_ref[...]).astype(v_ref.dtype)


def _kv(x, cos_t, sin_t, w, tm):
    bsz, s_len, d = x.shape
    lat = w["kv_latent_g"].shape[1]
    nk = w["kv_wk"].shape[1]
    nv = w["kv_wv"].shape[1]
    tok = lambda n: pl.BlockSpec((1, tm, n), lambda b, j: (b, j, 0))
    names = ["kv_norm_g", "kv_wd", "kv_latent_g", "kv_wk", "kv_wv"]
    return pl.pallas_call(
        functools.partial(_kv_kernel, lat),
        grid=(bsz, s_len // tm),
        in_specs=[tok(d), tok(MLA_PAD), tok(MLA_PAD)] + [_const_spec(w[n].shape) for n in names],
        out_specs=[tok(nk), tok(nv)],
        out_shape=[jax.ShapeDtypeStruct((bsz, s_len, nk), BF16),
                   jax.ShapeDtypeStruct((bsz, s_len, nv), BF16)],
        compiler_params=_params(("parallel", "parallel")),
        name="mla_kv",
    )(x, cos_t, sin_t, *[w[n] for n in names])


def _b_pre_kernel(q_lora, scale, x_ref, cos_ref, sin_ref, g_ref, w_in_ref, qg_ref, qa_ref, qb_ref,
                  q_ref, qmem_ref):
    h = _rms(x_ref[0], g_ref[...])
    proj = _dot(h, w_in_ref[...])
    qmem_ref[0] = proj[:, q_lora:]
    cq = _rms(proj[:, :q_lora], qg_ref[...]).astype(BF16)
    qa = jnp.dot(cq, qa_ref[...], preferred_element_type=F32)
    qb = jnp.dot(cq, qb_ref[...], preferred_element_type=F32)
    heads = qa.shape[1] // MLA_PAD
    cos_t = jnp.concatenate([cos_ref[0]] * heads, axis=1)
    sin_t = jnp.concatenate([sin_ref[0]] * heads, axis=1)
    q_ref[0] = ((qa * cos_t + qb * sin_t) * scale).astype(q_ref.dtype)


def _b_pre(x, cos_t, sin_t, w, tm):
    bsz, s_len, d = x.shape
    q_lora = w["q_norm_g"].shape[1]
    mem_dim = w["w_in"].shape[1] - q_lora
    nq = w["q_up_a"].shape[1]
    scale = (QK_NOPE + QK_ROPE) ** -0.5 * float(np.log2(np.e))
    tok = lambda n: pl.BlockSpec((1, tm, n), lambda b, j: (b, j, 0))
    names = ["norm_g", "w_in", "q_norm_g", "q_up_a", "q_up_b"]
    return pl.pallas_call(
        functools.partial(_b_pre_kernel, q_lora, scale),
        grid=(bsz, s_len // tm),
        in_specs=[tok(d), tok(MLA_PAD), tok(MLA_PAD)] + [_const_spec(w[n].shape) for n in names],
        out_specs=[tok(nq), tok(mem_dim)],
        out_shape=[jax.ShapeDtypeStruct((bsz, s_len, nq), BF16),
                   jax.ShapeDtypeStruct((bsz, s_len, mem_dim), F32)],
        compiler_params=_params(("parallel", "parallel")),
        name="mla_q",
    )(x, cos_t, sin_t, *[w[n] for n in names])


def _flash_kernel(t, q_ref, k_ref, v_ref, o_ref, m_ref, l_ref, acc_ref, sa_ref, sb_ref):
    i = pl.program_id(2)
    half = t // 2
    split = half % 128 == 0
    m_ref[...] = jnp.full_like(m_ref, -jnp.inf)
    l_ref[...] = jnp.zeros_like(l_ref)
    acc_ref[...] = jnp.zeros_like(acc_ref)
    low = lax.broadcasted_iota(jnp.int32, (1, 2 * HEAD), 1) < HEAD
    diag_blocks = [(0, half, half), (half, t, t)] if split else [(0, t, t)]

    def key_rows(j):
        return pl.ds(pl.multiple_of(j * t, t), t)

    def scores(j, s_ref, blocks):
        for hd in range(2):
            for r0, r1, nc in blocks:
                q = q_ref[0, r0:r1, hd * MLA_PAD:(hd + 1) * MLA_PAD]
                k = k_ref[0, pl.ds(pl.multiple_of(j * t, t), nc), hd * MLA_PAD:(hd + 1) * MLA_PAD]
                s_ref[hd, r0:r1, 0:nc] = lax.dot_general(q, k, (((1,), (1,)), ((), ())),
                                                        preferred_element_type=F32)

    def apply(j, s_ref, blocks, diagonal):
        vpair = v_ref[0, key_rows(j), :]
        vhead = (jnp.where(low, vpair, 0), jnp.where(low, 0, vpair))
        for r0, r1, nc in blocks:
            nr = r1 - r0
            if diagonal:
                visible = (lax.broadcasted_iota(jnp.int32, (nr, nc), 0) + r0
                           >= lax.broadcasted_iota(jnp.int32, (nr, nc), 1))
            alphas, pvs = [], []
            for hd in range(2):
                s = s_ref[hd, r0:r1, 0:nc]
                if diagonal:
                    s = jnp.where(visible, s, -jnp.inf)
                m_prev = m_ref[hd, r0:r1, :]
                m_new = jnp.maximum(m_prev, jnp.max(s, axis=-1, keepdims=True))
                alpha = jnp.exp2(m_prev - m_new)
                p = jnp.exp2(s - jnp.concatenate([m_new] * (nc // 128), axis=1))
                part = p[:, 0:128]
                for c in range(1, nc // 128):
                    part = part + p[:, c * 128:(c + 1) * 128]
                l_ref[hd, r0:r1, :] = alpha * l_ref[hd, r0:r1, :] + part
                m_ref[hd, r0:r1, :] = m_new
                pvs.append(jnp.dot(p.astype(BF16), vhead[hd][0:nc], preferred_element_type=F32))
                alphas.append(alpha)
            acc_ref[r0:r1, :] = (acc_ref[r0:r1, :] * jnp.where(low, alphas[0], alphas[1])
                                 + pvs[0] + pvs[1])

    full = [(0, t, t)]

    scores(0, sa_ref, full)

    def two_tiles(n, carry):
        scores(2 * n + 1, sb_ref, full)
        apply(2 * n, sa_ref, full, False)

        scores(2 * n + 2, sa_ref, full)
        apply(2 * n + 1, sb_ref, full, False)
        return carry

    lax.fori_loop(0, i // 2, two_tiles, 0)

    @pl.when(i % 2 == 1)
    def _():
        scores(i, sb_ref, diag_blocks)
        apply(i - 1, sa_ref, full, False)
        apply(i, sb_ref, diag_blocks, True)

    @pl.when(i % 2 == 0)
    def _():
        apply(i, sa_ref, diag_blocks, True)

    l0 = jnp.sum(l_ref[0], axis=-1, keepdims=True)
    l1 = jnp.sum(l_ref[1], axis=-1, keepdims=True)
    o_ref[0] = (acc_ref[...] / jnp.where(low, l0, l1)).astype(o_ref.dtype)


def _flash(q, k, v, t):
    bsz, s_len, nq = q.shape
    pairs = nq // (2 * MLA_PAD)
    return pl.pallas_call(
        functools.partial(_flash_kernel, t),
        grid=(bsz, pairs, s_len // t),
        in_specs=[pl.BlockSpec((1, t, 2 * MLA_PAD), lambda b, p, i: (b, i, p)),
                  pl.BlockSpec((1, s_len, 2 * MLA_PAD), lambda b, p, i: (b, 0, p)),
                  pl.BlockSpec((1, s_len, 2 * HEAD), lambda b, p, i: (b, 0, p))],
        out_specs=pl.BlockSpec((1, t, 2 * HEAD), lambda b, p, i: (b, i, p)),
        out_shape=jax.ShapeDtypeStruct((bsz, s_len, pairs * 2 * HEAD), BF16),
        scratch_shapes=[pltpu.VMEM((2, t, 128), F32), pltpu.VMEM((2, t, 128), F32),
                        pltpu.VMEM((t, 2 * HEAD), F32),
                        pltpu.VMEM((2, t, t), F32), pltpu.VMEM((2, t, t), F32)],
        compiler_params=_params(("parallel", "parallel", "arbitrary")),
        name="mla_flash",
    )(q, k, v)


def _row(v):
    return v.reshape(1, -1).astype(F32)


def _pad_rows(w, total, offset):
    out = jnp.zeros((total, w.shape[1]), w.dtype)
    return out.at[offset:offset + w.shape[0]].set(w)


def _block_ones(n):
    h = np.arange(n) // HEAD
    return jnp.asarray(h[:, None] == h[None, :], BF16)


def _rope_cols(w_rope, swap):
    half = QK_ROPE // 2
    t1, t2 = w_rope[:, :half], w_rope[:, half:]
    if swap:
        t1, t2 = t2, t1
    z = lambda n: jnp.zeros((w_rope.shape[0], n), w_rope.dtype)
    return jnp.concatenate([z(QK_NOPE), t1, t2, z(MLA_PAD - QK_NOPE - QK_ROPE)], axis=1)


def _q_up_padded(q_up, heads, swap):
    per = QK_NOPE + QK_ROPE
    blocks = []
    for hd in range(heads):
        wh = q_up[:, hd * per:(hd + 1) * per]
        rope = _rope_cols(wh[:, QK_NOPE:], swap)
        if swap:
            blocks.append(rope)
        else:
            blocks.append(rope.at[:, :QK_NOPE].set(wh[:, :QK_NOPE]))
    return jnp.concatenate(blocks, axis=1).astype(BF16)


def kernel(x, mem, positions, mem_norm_g, a_norm1_g, a_w_in, a_shift_mu, a_decay_up, a_decay_bias, a_aaa_up, a_aaa_bias, a_gate_up, a_k_k, a_k_a, a_r_k, a_lnx_g, a_lnx_b, a_mem_kv, a_w_out, a_norm2_g, a_ffn_gu, a_ffn_down, vres_mu, vres_down, vres_up, vres_bias, kv_norm_g, kv_w_down, kv_latent_g, kv_w_up, b_norm1_g, b_w_in, b_q_norm_g, b_q_up, b_mem_kv, b_w_out, b_norm2_g, b_ffn_gu, b_ffn_down, final_norm_g):
    bsz, s_len, d = x.shape
    n_a = a_w_in.shape[0]
    n_b = b_w_in.shape[0]
    rdim = a_k_k.shape[1]
    assert CHUNK == HEAD and rdim % GL == 0 and s_len % CHUNK == 0
    tm = _tile(s_len, TILES["tm"])
    tc = _tile(s_len, TILES["tc"])
    tq = _tile(s_len, TILES["tq"])

    half = QK_ROPE // 2
    inv_freq = 10000.0 ** (-jnp.arange(half, dtype=F32) / half)
    ang = positions.astype(F32)[..., None] * inv_freq
    cos, sin = lax.optimization_barrier((jnp.cos(ang), jnp.sin(ang)))
    ones = jnp.ones((bsz, s_len, QK_NOPE), F32)
    zpad = jnp.zeros((bsz, s_len, MLA_PAD - QK_NOPE - QK_ROPE), F32)
    cos_t = jnp.concatenate([ones, cos, cos, zpad], axis=-1)
    sin_t = jnp.concatenate([0 * ones, -sin, sin, zpad], axis=-1)

    memkv = _memkv(mem, _row(mem_norm_g),
                   jnp.concatenate(list(a_mem_kv) + list(b_mem_kv), axis=1).astype(BF16))
    bd768 = _block_ones(rdim)

    lora = a_decay_up.shape[1] + a_aaa_up.shape[1]
    v_first = None
    for i in range(n_a):
        w = dict(norm_g=_row(a_norm1_g[i]), w_in=a_w_in[i].astype(BF16), mu=_row(a_shift_mu[i]),
                 decay_up=_pad_rows(a_decay_up[i], lora, 0).astype(BF16),
                 decay_bias=_row(a_decay_bias[i]),
                 aaa_up=_pad_rows(a_aaa_up[i], lora, a_decay_up.shape[1]).astype(BF16),
                 aaa_bias=_row(a_aaa_bias[i]), gate_up=a_gate_up[i].astype(BF16),
                 k_k=_row(a_k_k[i]), k_a=_row(a_k_a[i]), bd768=bd768)
        if i > 0:
            w.update(vres_mu=_row(vres_mu[i - 1]),
                     vres_down=jnp.pad(vres_down[i - 1], ((0, 0), (0, 128 - vres_down.shape[2]))).astype(BF16),
                     vres_up=_pad_rows(vres_up[i - 1], 128, 0).astype(BF16),
                     vres_bias=_row(vres_bias[i - 1]))
        r, lw, k, v, a, b, gate, qmem = _a_pre(x, v_first if i > 0 else None, w, tm)
        if i == 0:
            v_first = v
        y_mix = _scan(r, lw, k, v, a, b, gate, _row(a_lnx_g[i]), _row(a_lnx_b[i]),
                      _row(a_r_k[i]), tc)
        wp = dict(w_out=a_w_out[i].astype(BF16), norm2_g=_row(a_norm2_g[i]),
                  ffn_gu=a_ffn_gu[i].astype(BF16), ffn_down=a_ffn_down[i].astype(BF16))
        x = _post(x, y_mix, qmem, memkv, i, wp, None, tm)

    lat = kv_latent_g.shape[0]
    heads = b_q_up.shape[2] // (QK_NOPE + QK_ROPE)
    per_kv = kv_w_up.shape[1] // heads
    wk_blocks, wv_blocks = [], []
    for hd in range(heads):
        blk = kv_w_up[:, hd * per_kv:(hd + 1) * per_kv]
        wk_blocks.append(jnp.pad(blk[:, :QK_NOPE], ((0, 0), (0, MLA_PAD - QK_NOPE))))
        wv_blocks.append(blk[:, QK_NOPE:])
    wkv = dict(kv_norm_g=_row(kv_norm_g),
               kv_wd=jnp.concatenate([kv_w_down[:, :lat], _rope_cols(kv_w_down[:, lat:], False),
                                      _rope_cols(kv_w_down[:, lat:], True)], axis=1).astype(BF16),
               kv_latent_g=_row(kv_latent_g),
               kv_wk=jnp.concatenate(wk_blocks, axis=1).astype(BF16),
               kv_wv=jnp.concatenate(wv_blocks, axis=1).astype(BF16))
    k_all, v_all = _kv(x, cos_t, sin_t, wkv, tm)

    for j in range(n_b):
        wq = dict(norm_g=_row(b_norm1_g[j]), w_in=b_w_in[j].astype(BF16),
                  q_norm_g=_row(b_q_norm_g[j]),
                  q_up_a=_q_up_padded(b_q_up[j], heads, False),
                  q_up_b=_q_up_padded(b_q_up[j], heads, True))
        q, qmem = _b_pre(x, cos_t, sin_t, wq, tm)
        o = _flash(q, k_all, v_all, tq)
        wp = dict(w_out=b_w_out[j].astype(BF16), norm2_g=_row(b_norm2_g[j]),
                  ffn_gu=b_ffn_gu[j].astype(BF16), ffn_down=b_ffn_down[j].astype(BF16))
        x = _post(x, o, qmem, memkv, n_a + j, wp,
                  _row(final_norm_g) if j == n_b - 1 else None, tm)
    return x
```

```python
import functools

import numpy as np
import jax
import jax.numpy as jnp
from jax import lax
from jax.experimental import pallas as pl
from jax.experimental.pallas import tpu as pltpu

F32 = jnp.float32
BF16 = jnp.bfloat16

HEAD = 64
CHUNK = 64
GROUP = 4
MXU_TILE = 256
GL = GROUP * HEAD
NORM_EPS = 1e-6
LNX_EPS = 64e-5
QK_NOPE = 64
QK_ROPE = 32
MLA_PAD = 128
VMEM_LIMIT = 56 * 1024 * 1024
TILES = dict(tm=512, tc=512, tq=1024)


def _dot(a, b):
    return jnp.dot(a.astype(BF16), b.astype(BF16), preferred_element_type=F32)


def _dot_nt(a, b):
    return lax.dot_general(a.astype(BF16), b.astype(BF16), (((1,), (1,)), ((), ())),
                           preferred_element_type=F32)


def _dot_tn(a, b):
    return lax.dot_general(a.astype(BF16), b.astype(BF16), (((0,), (0,)), ((), ())),
                           preferred_element_type=F32)


def _split(x, terms):
    out = []
    for _ in range(terms):
        p = x.astype(BF16)
        out.append(p)
        x = x - p.astype(F32)
    return out


def _dot_ones_lhs(ones_bf16, x, terms=3):
    acc = None
    for p in _split(x, terms):
        d = jnp.dot(ones_bf16, p, preferred_element_type=F32)
        acc = d if acc is None else acc + d
    return acc


def _rms(x, g, eps=NORM_EPS):
    ms = jnp.mean(x * x, axis=-1, keepdims=True)
    return x * lax.rsqrt(ms + eps) * g


def _sigmoid(x):
    return 1.0 / (1.0 + jnp.exp(-x))


def _shift_rows(x, carry_ref):
    tm = x.shape[0]
    first = lax.broadcasted_iota(jnp.int32, (tm, 1), 0) == 0
    prev = jnp.where(first, carry_ref[0:1, :], pltpu.roll(x, 1, 0))
    carry_ref[0:1, :] = x[tm - 1:tm, :]
    return prev


def _const_spec(shape):
    nd = len(shape)
    return pl.BlockSpec(shape, lambda *_: (0,) * nd, pipeline_mode=pl.Buffered(1))


def _params(sem):
    return pltpu.CompilerParams(dimension_semantics=sem, vmem_limit_bytes=VMEM_LIMIT)


def _tile(n, pref):
    t = min(n, pref)
    assert n % t == 0, (n, t)
    return t


def _memkv_kernel(mem_ref, g_ref, w_ref, o_ref):
    mn = _rms(mem_ref[0], g_ref[...])
    o_ref[0] = _dot(mn, w_ref[...]).astype(o_ref.dtype)


def _memkv(mem, g, w_all):
    bsz, n_mem, d = mem.shape
    n = w_all.shape[1]
    return pl.pallas_call(
        _memkv_kernel,
        grid=(bsz,),
        in_specs=[pl.BlockSpec((1, n_mem, d), lambda b: (b, 0, 0)),
                  _const_spec((1, d)), _const_spec((d, n))],
        out_specs=pl.BlockSpec((1, n_mem, n), lambda b: (b, 0, 0)),
        out_shape=jax.ShapeDtypeStruct((bsz, n_mem, n), BF16),
        compiler_params=_params(("parallel",)),
        name="memkv",
    )(mem, g, w_all)


def _a_pre_kernel(has_vres, rdim, rwkv_in, lora, *refs):
    it = iter(refs)
    x_ref = next(it)
    vfirst_ref = next(it) if has_vres else None
    g_ref, w_in_ref, mu_ref = next(it), next(it), next(it)
    dup_ref, dbias_ref, aup_ref, abias_ref, gup_ref = (next(it) for _ in range(5))
    kk_ref, ka_ref, bd_ref = next(it), next(it), next(it)
    if has_vres:
        vmu_ref, vdown_ref, vup_ref, vbias_ref = (next(it) for _ in range(4))
    r_ref, lw_ref, k_ref, v_ref, a_ref, b_ref, gate_ref, qmem_ref = (next(it) for _ in range(8))
    carry_p = next(it)
    carry_h = next(it) if has_vres else None

    @pl.when(pl.program_id(1) == 0)
    def _():
        carry_p[...] = jnp.zeros_like(carry_p)
        if has_vres:
            carry_h[...] = jnp.zeros_like(carry_h)

    h = _rms(x_ref[0], g_ref[...])
    proj = _dot(h, w_in_ref[...])
    qmem_ref[0] = proj[:, rwkv_in:]
    p = proj[:, :rwkv_in]
    pm = p + (_shift_rows(p, carry_p) - p) * mu_ref[...]

    r = pm[:, 0:rdim]
    k = pm[:, rdim:2 * rdim]
    v = pm[:, 2 * rdim:3 * rdim]
    lo = pm[:, 3 * rdim:3 * rdim + lora]
    g_lo = pm[:, 3 * rdim + lora:]

    z = dbias_ref[...] + _dot(jnp.tanh(lo), dup_ref[...])
    nz = -z
    softplus = jnp.maximum(nz, 0.0) + jnp.log1p(jnp.exp(-jnp.abs(nz)))
    log_w = -softplus - 0.5
    lw_ref[0] = -jnp.exp(log_w)
    lr = _sigmoid(abias_ref[...] + _dot(lo, aup_ref[...]))
    gate_ref[0] = _dot(_sigmoid(g_lo), gup_ref[...])

    if has_vres:
        hv = h + (_shift_rows(h, carry_h) - h) * vmu_ref[...]
        t = _dot(hv, vdown_ref[...])
        sg = _sigmoid(vbias_ref[...] + _dot(t, vup_ref[...]))
        v = v + (vfirst_ref[0] - v) * sg

    kkr = k * kk_ref[...]
    sq = kkr * kkr
    norm = jnp.sqrt(jnp.concatenate(
        [_dot(sq[:, g:g + GL], bd_ref[...]) for g in range(0, rdim, GL)], axis=1))
    kk = kkr / jnp.maximum(norm, 1e-12)
    r_ref[0] = r
    k_ref[0] = k * (1.0 + (lr - 1.0) * ka_ref[...])
    v_ref[0] = v
    a_ref[0] = -kk
    b_ref[0] = kk * lr


def _a_pre(x, vfirst, w, tm):
    bsz, s_len, d = x.shape
    rdim = w["k_k"].shape[1]
    n_in = w["w_in"].shape[1]
    rwkv_in = w["mu"].shape[1]
    mem_dim = n_in - rwkv_in
    has_vres = vfirst is not None
    tok = lambda n: pl.BlockSpec((1, tm, n), lambda b, j: (b, j, 0))
    args, specs = [x], [tok(d)]
    if has_vres:
        args.append(vfirst)
        specs.append(tok(rdim))
    names = ["norm_g", "w_in", "mu", "decay_up", "decay_bias", "aaa_up", "aaa_bias", "gate_up",
             "k_k", "k_a", "head_ones"]
    if has_vres:
        names += ["vres_mu", "vres_down", "vres_up", "vres_bias"]
    for n in names:
        args.append(w[n])
        specs.append(_const_spec(w[n].shape))
    scratch = [pltpu.VMEM((8, rwkv_in), F32)]
    if has_vres:
        scratch.append(pltpu.VMEM((8, d), F32))
    out_shape = [jax.ShapeDtypeStruct((bsz, s_len, rdim), F32)] * 7 + \
                [jax.ShapeDtypeStruct((bsz, s_len, mem_dim), F32)]
    out_specs = [tok(rdim)] * 7 + [tok(mem_dim)]
    return pl.pallas_call(
        functools.partial(_a_pre_kernel, has_vres, rdim, rwkv_in, w["decay_up"].shape[0]),
        grid=(bsz, s_len // tm),
        in_specs=specs, out_specs=out_specs, out_shape=out_shape,
        scratch_shapes=scratch,
        compiler_params=_params(("parallel", "arbitrary")),
        name="a_pre_vres" if has_vres else "a_pre",
    )(*args)


def _stack_heads(x_bf16, hm_ref):
    return jnp.concatenate([x_bf16 * hm_ref[h] for h in range(GROUP)], axis=0)


def _scan_prepare(ins, consts, done):
    tri, hm_ref, mask2, icat, bd = consts
    c = CHUNK
    gc = GROUP * c
    each = range(len(ins))
    bd_b = bd[...]
    r, lw, k, a, b = zip(*ins)

    cl = [_dot_ones_lhs(tri[...], lw[g]) for g in each]
    yield
    w_end = [jnp.exp(cl[g][c - 1:c, :]) for g in each]
    e_inc = [jnp.exp(cl[g]) for g in each]
    e_exc = [jnp.exp(cl[g] - lw[g]) for g in each]
    e_inv = [jnp.exp(-cl[g]) for g in each]
    kt = [k[g] * e_inv[g] for g in each]
    bt = [b[g] * e_inv[g] for g in each]
    ar = [jnp.concatenate([a[g] * e_exc[g], r[g] * e_inc[g]], axis=0).astype(BF16) for g in each]
    bks = [jnp.concatenate([_stack_heads(bt[g].astype(BF16), hm_ref),
                            _stack_heads(kt[g].astype(BF16), hm_ref)], axis=0) for g in each]
    m = [_dot_nt(ar[g], bks[g]) * mask2[...] for g in each]
    l_ab = [m[g][:c, :gc] for g in each]
    bkh = [jnp.concatenate([bt[g] * w_end[g], kt[g] * w_end[g]], axis=0).astype(BF16) for g in each]
    yield

    def blockdiag(p_bf16):
        return jnp.concatenate([p_bf16] * GROUP, axis=0) * bd_b

    pw = [l_ab[g].astype(BF16) for g in each]
    pw = [jnp.dot(pw[g], blockdiag(pw[g]), preferred_element_type=F32) for g in each]
    inv = [icat[...] + l_ab[g] for g in each]
    yield
    steps = int(np.log2(c)) - 1
    for it in range(steps):
        pbd = [blockdiag(pw[g].astype(BF16)) for g in each]
        if it + 1 < steps:
            both = [jnp.dot(jnp.concatenate([inv[g], pw[g]], axis=0).astype(BF16), pbd[g],
                            preferred_element_type=F32) for g in each]
            inv = [inv[g] + both[g][:c] for g in each]
            pw = [both[g][c:] for g in each]
        else:
            inv = [inv[g] + jnp.dot(inv[g].astype(BF16), pbd[g], preferred_element_type=F32)
                   for g in each]
        yield
    done(ar, [m[g][:c, gc:].astype(BF16) for g in each], [m[g][c:, :].astype(BF16) for g in each],
         [inv[g].astype(BF16) for g in each], bkh, w_end)


def _scan_advance(pre, vb, states, hm_ref, bd_f, done):
    c = CHUNK
    each = range(len(states))
    ar, l_ak, m_r, inv, bkh, w_end = pre
    ars = [_dot_nt(ar[g], states[g]) for g in each]
    vs = [_stack_heads(vb[g], hm_ref) for g in each]
    z = [ars[g][:c] + jnp.dot(l_ak[g], vs[g], preferred_element_type=F32) for g in each]
    yield
    u = [jnp.dot(inv[g], _stack_heads(z[g].astype(BF16), hm_ref), preferred_element_type=F32)
         for g in each]
    ub = [u[g].astype(BF16) for g in each]
    yield
    y = [ars[g][c:] + jnp.dot(m_r[g], jnp.concatenate([_stack_heads(ub[g], hm_ref), vs[g]], axis=0),
                              preferred_element_type=F32) for g in each]
    s_new = [states[g] * w_end[g]
             + bd_f * _dot_tn(jnp.concatenate([ub[g], vb[g]], axis=0), bkh[g]) for g in each]
    done(y, s_new)


def _weave(main, *others):
    live = [main, *others]
    while live:
        for g in list(live):
            if next(g, live) is live:
                live.remove(g)


def _scan_kernel(n_groups, n_chunks, r_ref, lw_ref, k_ref, v_ref, a_ref, b_ref, gate_ref,
                 lnxg_ref, lnxb_ref, rk_ref, tri_ref, hm_ref, mask2_ref, icat_ref, bd_ref,
                 o_ref, state_ref, ar_scr, lak_scr, mr_scr, inv_scr, bkh_scr, wend_scr, y_scr):
    @pl.when(pl.program_id(1) == 0)
    def _():
        state_ref[...] = jnp.zeros_like(state_ref)

    c = CHUNK
    gc = GROUP * c
    consts = (tri_ref, hm_ref, mask2_ref, icat_ref, bd_ref)
    seqs = [(bi, g) for bi in range(r_ref.shape[0]) for g in range(n_groups)]
    lanes = [slice(g * GL, (g + 1) * GL) for g in range(n_groups)]
    cat = [slice(g * gc, (g + 1) * gc) for g in range(n_groups)]
    cat2 = [slice(g * 2 * gc, (g + 1) * 2 * gc) for g in range(n_groups)]
    bd_b = bd_ref[...]
    bd_f = bd_b.astype(F32)

    def rows(ci, n=c):
        return slice(ci * n, (ci + 1) * n)

    def prepare(ci):
        ins = [tuple(ref[bi, rows(ci), lanes[g]] for ref in (r_ref, lw_ref, k_ref, a_ref, b_ref))
               for bi, g in seqs]

        def done(ar, l_ak, m_r, inv, bkh, w_end):
            for n, (bi, g) in enumerate(seqs):
                ar_scr[bi, rows(ci, 2 * c), lanes[g]] = ar[n]
                lak_scr[bi, rows(ci), cat[g]] = l_ak[n]
                mr_scr[bi, rows(ci), cat2[g]] = m_r[n]
                inv_scr[bi, rows(ci), cat[g]] = inv[n]
                bkh_scr[bi, rows(ci, 2 * c), lanes[g]] = bkh[n]
                wend_scr[bi, ci, :, lanes[g]] = w_end[n]

        return _scan_prepare(ins, consts, done)

    def advance(ci):
        pre = ([ar_scr[bi, rows(ci, 2 * c), lanes[g]] for bi, g in seqs],
               [lak_scr[bi, rows(ci), cat[g]] for bi, g in seqs],
               [mr_scr[bi, rows(ci), cat2[g]] for bi, g in seqs],
               [inv_scr[bi, rows(ci), cat[g]] for bi, g in seqs],
               [bkh_scr[bi, rows(ci, 2 * c), lanes[g]] for bi, g in seqs],
               [wend_scr[bi, ci, :, lanes[g]] for bi, g in seqs])
        vb = [v_ref[bi, rows(ci), lanes[g]].astype(BF16) for bi, g in seqs]

        def done(y, s_new):
            for n, (bi, g) in enumerate(seqs):
                state_ref[n] = s_new[n]
                y_scr[bi, rows(ci), lanes[g]] = y[n]

        return _scan_advance(pre, vb, [state_ref[n] for n in range(len(seqs))], hm_ref, bd_f, done)

    _weave(prepare(0))
    for ci in range(n_chunks):
        _weave(advance(ci), *([prepare(ci + 1)] if ci + 1 < n_chunks else []))

    bb, tc = y_scr.shape[0], y_scr.shape[1]
    flat = lambda t: t.reshape(bb * tc, GL)
    for ln in lanes:
        y = flat(y_scr[:, :, ln])
        mean = _dot(y, bd_b) * (1.0 / HEAD)
        d = y - mean
        var = _dot(d * d, bd_b) * (1.0 / HEAD)
        v = flat(v_ref[:, :, ln])
        bonus = _dot(flat(r_ref[:, :, ln]) * flat(k_ref[:, :, ln]) * rk_ref[:, ln], bd_b) * v
        gn = d * lax.rsqrt(var + LNX_EPS) * lnxg_ref[:, ln] + lnxb_ref[:, ln]
        out = (gn + bonus) * flat(gate_ref[:, :, ln])
        o_ref[:, :, ln] = out.reshape(bb, tc, GL).astype(o_ref.dtype)


def _scan_consts():
    c, g = CHUNK, GROUP
    gc = g * c
    i = np.arange(c)[:, None]
    j = np.arange(gc)[None, :] % c
    strict = (j < i).astype(np.float32)
    incl = (j <= i).astype(np.float32)
    mask2 = np.concatenate([np.concatenate([strict, strict], 1),
                            np.concatenate([incl, incl], 1)], 0)
    icat = (j == i).astype(np.float32)
    tri = (np.arange(c)[None, :] <= np.arange(c)[:, None]).astype(np.float32)
    lane_head = np.arange(GL) // HEAD
    hm = (lane_head[None, None, :] == np.arange(g)[:, None, None]).astype(np.float32)
    bd = (lane_head[:, None] == lane_head[None, :]).astype(np.float32)
    return (jnp.asarray(tri, BF16), jnp.asarray(hm, BF16), jnp.asarray(mask2, F32),
            jnp.asarray(icat, F32), jnp.asarray(bd, BF16))


def _scan(r, lw, k, v, a, b, gate, lnx_g, lnx_b, r_k, rows_per_step):
    bsz, s_len, rdim = r.shape
    n_groups = rdim // GL
    bb = next(n for n in (4, 2, 1) if bsz % n == 0)
    tc = max(CHUNK, rows_per_step // bb)
    assert s_len % tc == 0
    n_chunks = tc // CHUNK
    gc = GROUP * CHUNK
    tok = pl.BlockSpec((bb, tc, rdim), lambda bi, j: (bi, j, 0))
    consts = _scan_consts()
    small = [lnx_g, lnx_b, r_k] + list(consts)
    return pl.pallas_call(
        functools.partial(_scan_kernel, n_groups, n_chunks),
        grid=(bsz // bb, s_len // tc),
        in_specs=[tok] * 7 + [_const_spec(t.shape) for t in small],
        out_specs=tok,
        out_shape=jax.ShapeDtypeStruct((bsz, s_len, rdim), BF16),
        scratch_shapes=[pltpu.VMEM((bb * n_groups, GL, GL), F32),
                        pltpu.VMEM((bb, 2 * tc, rdim), BF16),
                        pltpu.VMEM((bb, tc, n_groups * gc), BF16),
                        pltpu.VMEM((bb, tc, n_groups * 2 * gc), BF16),
                        pltpu.VMEM((bb, tc, n_groups * gc), BF16),
                        pltpu.VMEM((bb, 2 * tc, rdim), BF16),
                        pltpu.VMEM((bb, n_chunks, 1, rdim), F32),
                        pltpu.VMEM((bb, tc, rdim), F32)],
        compiler_params=_params(("parallel", "arbitrary")),
        name="rwkv_scan",
    )(r, lw, k, v, a, b, gate, *small)


def _ffn_slices(fh):
    n = fh // MXU_TILE
    if fh % MXU_TILE or n < 2:
        return ((0, fh),)
    mid = (n + 1) // 2 * MXU_TILE
    return ((0, mid), (mid, fh))


def _post_kernel(mem_heads, final, x_ref, mix_ref, qmem_ref, mk_ref, mv_ref, hm_ref,
                 w_out_ref, g2_ref, w_gu_ref, w_dn_ref, gf_ref, o_ref):
    mix_dim = mix_ref.shape[2]
    fh = w_dn_ref.shape[0]

    qm = (qmem_ref[0] * (HEAD ** -0.5)).astype(BF16)
    mk = mk_ref[0]
    mv = mv_ref[0]
    m = None
    for hd in range(mem_heads):
        s = _dot_nt(qm * hm_ref[hd], mk)
        s = s - jnp.max(s, axis=-1, keepdims=True)
        e = jnp.exp(s)
        p = e / jnp.sum(e, axis=-1, keepdims=True)
        o = _dot(p, mv) * hm_ref[hd].astype(F32)
        m = o if m is None else m + o

    x1 = x_ref[0] + _dot(mix_ref[0], w_out_ref[0:mix_dim, :]) + _dot(m, w_out_ref[mix_dim:, :])
    h2 = _rms(x1, g2_ref[...]).astype(BF16)
    acc = None
    for h0, h1 in _ffn_slices(fh):
        gt = jnp.dot(h2, w_gu_ref[:, h0:h1], preferred_element_type=F32)
        ut = jnp.dot(h2, w_gu_ref[:, fh + h0:fh + h1], preferred_element_type=F32)
        act = gt * _sigmoid(gt) * ut
        dt = _dot(act, w_dn_ref[h0:h1, :])
        acc = dt if acc is None else acc + dt
    x2 = x1 + acc
    if final:
        x2 = _rms(x2, gf_ref[...])
    o_ref[0] = x2


def _post(x, mix, qmem, memkv, layer, w, final_g, tm):
    bsz, s_len, d = x.shape
    mix_dim = mix.shape[2]
    mem_dim = qmem.shape[2]
    n_mem = memkv.shape[1]
    mem_heads = mem_dim // HEAD
    lane_head = np.arange(mem_dim) // HEAD
    hm = jnp.asarray((lane_head[None, None, :] == np.arange(mem_heads)[:, None, None]), BF16)
    tok = lambda n: pl.BlockSpec((1, tm, n), lambda b, j: (b, j, 0))
    final = final_g is not None
    gf = final_g if final else w["norm2_g"]
    return pl.pallas_call(
        functools.partial(_post_kernel, mem_heads, final),
        grid=(bsz, s_len // tm),
        in_specs=[tok(d), tok(mix_dim), tok(mem_dim),
                  pl.BlockSpec((1, n_mem, mem_dim), lambda b, j: (b, 0, 2 * layer)),
                  pl.BlockSpec((1, n_mem, mem_dim), lambda b, j: (b, 0, 2 * layer + 1)),
                  _const_spec(hm.shape), _const_spec(w["w_out"].shape),
                  _const_spec(w["norm2_g"].shape), _const_spec(w["ffn_gu"].shape),
                  _const_spec(w["ffn_down"].shape), _const_spec(gf.shape)],
        out_specs=tok(d),
        out_shape=jax.ShapeDtypeStruct((bsz, s_len, d), F32),
        compiler_params=_params(("parallel", "parallel")),
        name="post_final" if final else "post",
    )(x, mix, qmem, memkv, memkv, hm, w["w_out"], w["norm2_g"], w["ffn_gu"], w["ffn_down"], gf)


def _kv_kernel(lat, x_ref, cos_ref, sin_ref, g_ref, wd_ref, lg_ref, wk_ref, wv_ref, k_ref, v_ref):
    hk = _rms(x_ref[0], g_ref[...])
    ckr = _dot(hk, wd_ref[...])
    ckv = _rms(ckr[:, :lat], lg_ref[...])
    krot = ckr[:, lat:lat + MLA_PAD] * cos_ref[0] + ckr[:, lat + MLA_PAD:] * sin_ref[0]
    kn = _dot(ckv, wk_ref[...])
    heads = kn.shape[1] // MLA_PAD
    k_ref[0] = (kn + jnp.concatenate([krot] * heads, axis=1)).astype(k_ref.dtype)
    v_ref[0] = _dot(ckv, wv_ref[...]).astype(v_ref.dtype)


def _kv(x, cos_t, sin_t, w, tm):
    bsz, s_len, d = x.shape
    lat = w["kv_latent_g"].shape[1]
    nk = w["kv_wk"].shape[1]
    nv = w["kv_wv"].shape[1]
    tok = lambda n: pl.BlockSpec((1, tm, n), lambda b, j: (b, j, 0))
    names = ["kv_norm_g", "kv_wd", "kv_latent_g", "kv_wk", "kv_wv"]
    return pl.pallas_call(
        functools.partial(_kv_kernel, lat),
        grid=(bsz, s_len // tm),
        in_specs=[tok(d), tok(MLA_PAD), tok(MLA_PAD)] + [_const_spec(w[n].shape) for n in names],
        out_specs=[tok(nk), tok(nv)],
        out_shape=[jax.ShapeDtypeStruct((bsz, s_len, nk), BF16),
                   jax.ShapeDtypeStruct((bsz, s_len, nv), BF16)],
        compiler_params=_params(("parallel", "parallel")),
        name="mla_kv",
    )(x, cos_t, sin_t, *[w[n] for n in names])


def _b_pre_kernel(q_lora, scale, x_ref, cos_ref, sin_ref, g_ref, w_in_ref, qg_ref, qa_ref, qb_ref,
                  q_ref, qmem_ref):
    h = _rms(x_ref[0], g_ref[...])
    proj = _dot(h, w_in_ref[...])
    qmem_ref[0] = proj[:, q_lora:]
    cq = _rms(proj[:, :q_lora], qg_ref[...]).astype(BF16)
    qa = jnp.dot(cq, qa_ref[...], preferred_element_type=F32)
    qb = jnp.dot(cq, qb_ref[...], preferred_element_type=F32)
    heads = qa.shape[1] // MLA_PAD
    cos_t = jnp.concatenate([cos_ref[0]] * heads, axis=1)
    sin_t = jnp.concatenate([sin_ref[0]] * heads, axis=1)
    q_ref[0] = ((qa * cos_t + qb * sin_t) * scale).astype(q_ref.dtype)


def _b_pre(x, cos_t, sin_t, w, tm):
    bsz, s_len, d = x.shape
    q_lora = w["q_norm_g"].shape[1]
    mem_dim = w["w_in"].shape[1] - q_lora
    nq = w["q_up_a"].shape[1]
    scale = (QK_NOPE + QK_ROPE) ** -0.5 * float(np.log2(np.e))
    tok = lambda n: pl.BlockSpec((1, tm, n), lambda b, j: (b, j, 0))
    names = ["norm_g", "w_in", "q_norm_g", "q_up_a", "q_up_b"]
    return pl.pallas_call(
        functools.partial(_b_pre_kernel, q_lora, scale),
        grid=(bsz, s_len // tm),
        in_specs=[tok(d), tok(MLA_PAD), tok(MLA_PAD)] + [_const_spec(w[n].shape) for n in names],
        out_specs=[tok(nq), tok(mem_dim)],
        out_shape=[jax.ShapeDtypeStruct((bsz, s_len, nq), BF16),
                   jax.ShapeDtypeStruct((bsz, s_len, mem_dim), F32)],
        compiler_params=_params(("parallel", "parallel")),
        name="mla_q",
    )(x, cos_t, sin_t, *[w[n] for n in names])


def _flash_kernel(t, q_ref, k_ref, v_ref, o_ref, m_ref, l_ref, acc_ref, sa_ref, sb_ref):
    i = pl.program_id(2)
    half = t // 2
    split = half % 128 == 0
    m_ref[...] = jnp.full_like(m_ref, -jnp.inf)
    l_ref[...] = jnp.zeros_like(l_ref)
    acc_ref[...] = jnp.zeros_like(acc_ref)
    low = lax.broadcasted_iota(jnp.int32, (1, 2 * HEAD), 1) < HEAD
    diag_blocks = [(0, half, half), (half, t, t)] if split else [(0, t, t)]

    def key_rows(j):
        return pl.ds(pl.multiple_of(j * t, t), t)

    def scores(j, s_ref, blocks):
        for hd in range(2):
            for r0, r1, nc in blocks:
                q = q_ref[0, r0:r1, hd * MLA_PAD:(hd + 1) * MLA_PAD]
                k = k_ref[0, pl.ds(pl.multiple_of(j * t, t), nc), hd * MLA_PAD:(hd + 1) * MLA_PAD]
                s_ref[hd, r0:r1, 0:nc] = lax.dot_general(q, k, (((1,), (1,)), ((), ())),
                                                        preferred_element_type=F32)

    def apply(j, s_ref, blocks, diagonal):
        vpair = v_ref[0, key_rows(j), :]
        vhead = (jnp.where(low, vpair, 0), jnp.where(low, 0, vpair))
        for r0, r1, nc in blocks:
            nr = r1 - r0
            if diagonal:
                visible = (lax.broadcasted_iota(jnp.int32, (nr, nc), 0) + r0
                           >= lax.broadcasted_iota(jnp.int32, (nr, nc), 1))
            alphas, pvs = [], []
            for hd in range(2):
                s = s_ref[hd, r0:r1, 0:nc]
                if diagonal:
                    s = jnp.where(visible, s, -jnp.inf)
                m_prev = m_ref[hd, r0:r1, :]
                m_new = jnp.maximum(m_prev, jnp.max(s, axis=-1, keepdims=True))
                alpha = jnp.exp2(m_prev - m_new)
                p = jnp.exp2(s - jnp.concatenate([m_new] * (nc // 128), axis=1))
                part = p[:, 0:128]
                for c in range(1, nc // 128):
                    part = part + p[:, c * 128:(c + 1) * 128]
                l_ref[hd, r0:r1, :] = alpha * l_ref[hd, r0:r1, :] + part
                m_ref[hd, r0:r1, :] = m_new
                pvs.append(jnp.dot(p.astype(BF16), vhead[hd][0:nc], preferred_element_type=F32))
                alphas.append(alpha)
            acc_ref[r0:r1, :] = (acc_ref[r0:r1, :] * jnp.where(low, alphas[0], alphas[1])
                                 + pvs[0] + pvs[1])

    full = [(0, t, t)]

    scores(0, sa_ref, full)

    def two_tiles(n, carry):
        scores(2 * n + 1, sb_ref, full)
        apply(2 * n, sa_ref, full, False)

        scores(2 * n + 2, sa_ref, full)
        apply(2 * n + 1, sb_ref, full, False)
        return carry

    lax.fori_loop(0, i // 2, two_tiles, 0)

    @pl.when(i % 2 == 1)
    def _():
        scores(i, sb_ref, diag_blocks)
        apply(i - 1, sa_ref, full, False)
        apply(i, sb_ref, diag_blocks, True)

    @pl.when(i % 2 == 0)
    def _():
        apply(i, sa_ref, diag_blocks, True)

    l0 = jnp.sum(l_ref[0], axis=-1, keepdims=True)
    l1 = jnp.sum(l_ref[1], axis=-1, keepdims=True)
    o_ref[0] = (acc_ref[...] / jnp.where(low, l0, l1)).astype(o_ref.dtype)


def _flash(q, k, v, t):
    bsz, s_len, nq = q.shape
    pairs = nq // (2 * MLA_PAD)
    return pl.pallas_call(
        functools.partial(_flash_kernel, t),
        grid=(bsz, pairs, s_len // t),
        in_specs=[pl.BlockSpec((1, t, 2 * MLA_PAD), lambda b, p, i: (b, i, p)),
                  pl.BlockSpec((1, s_len, 2 * MLA_PAD), lambda b, p, i: (b, 0, p)),
                  pl.BlockSpec((1, s_len, 2 * HEAD), lambda b, p, i: (b, 0, p))],
        out_specs=pl.BlockSpec((1, t, 2 * HEAD), lambda b, p, i: (b, i, p)),
        out_shape=jax.ShapeDtypeStruct((bsz, s_len, pairs * 2 * HEAD), BF16),
        scratch_shapes=[pltpu.VMEM((2, t, 128), F32), pltpu.VMEM((2, t, 128), F32),
                        pltpu.VMEM((t, 2 * HEAD), F32),
                        pltpu.VMEM((2, t, t), F32), pltpu.VMEM((2, t, t), F32)],
        compiler_params=_params(("parallel", "parallel", "arbitrary")),
        name="mla_flash",
    )(q, k, v)


def _row(v):
    return v.reshape(1, -1).astype(F32)


def _pad_rows(w, total, offset):
    out = jnp.zeros((total, w.shape[1]), w.dtype)
    return out.at[offset:offset + w.shape[0]].set(w)


def _block_ones(n):
    h = np.arange(n) // HEAD
    return jnp.asarray(h[:, None] == h[None, :], BF16)


def _rope_cols(w_rope, swap):
    half = QK_ROPE // 2
    t1, t2 = w_rope[:, :half], w_rope[:, half:]
    if swap:
        t1, t2 = t2, t1
    z = lambda n: jnp.zeros((w_rope.shape[0], n), w_rope.dtype)
    return jnp.concatenate([z(QK_NOPE), t1, t2, z(MLA_PAD - QK_NOPE - QK_ROPE)], axis=1)


def _q_up_padded(q_up, heads, swap):
    per = QK_NOPE + QK_ROPE
    blocks = []
    for hd in range(heads):
        wh = q_up[:, hd * per:(hd + 1) * per]
        rope = _rope_cols(wh[:, QK_NOPE:], swap)
        if swap:
            blocks.append(rope)
        else:
            blocks.append(rope.at[:, :QK_NOPE].set(wh[:, :QK_NOPE]))
    return jnp.concatenate(blocks, axis=1).astype(BF16)


def kernel(x, mem, positions, mem_norm_g, a_norm1_g, a_w_in, a_shift_mu, a_decay_up, a_decay_bias, a_aaa_up, a_aaa_bias, a_gate_up, a_k_k, a_k_a, a_r_k, a_lnx_g, a_lnx_b, a_mem_kv, a_w_out, a_norm2_g, a_ffn_gu, a_ffn_down, vres_mu, vres_down, vres_up, vres_bias, kv_norm_g, kv_w_down, kv_latent_g, kv_w_up, b_norm1_g, b_w_in, b_q_norm_g, b_q_up, b_mem_kv, b_w_out, b_norm2_g, b_ffn_gu, b_ffn_down, final_norm_g):
    bsz, s_len, d = x.shape
    n_a = a_w_in.shape[0]
    n_b = b_w_in.shape[0]
    rdim = a_k_k.shape[1]
    assert CHUNK == HEAD and rdim % GL == 0 and s_len % CHUNK == 0
    tm = _tile(s_len, TILES["tm"])
    tc = _tile(s_len, TILES["tc"])
    tq = _tile(s_len, TILES["tq"])

    half = QK_ROPE // 2
    inv_freq = 10000.0 ** (-jnp.arange(half, dtype=F32) / half)
    ang = positions.astype(F32)[..., None] * inv_freq
    cos, sin = lax.optimization_barrier((jnp.cos(ang), jnp.sin(ang)))
    ones = jnp.ones((bsz, s_len, QK_NOPE), F32)
    zpad = jnp.zeros((bsz, s_len, MLA_PAD - QK_NOPE - QK_ROPE), F32)
    cos_t = jnp.concatenate([ones, cos, cos, zpad], axis=-1)
    sin_t = jnp.concatenate([0 * ones, -sin, sin, zpad], axis=-1)

    memkv = _memkv(mem, _row(mem_norm_g),
                   jnp.concatenate(list(a_mem_kv) + list(b_mem_kv), axis=1).astype(BF16))
    head_ones = _block_ones(GL)

    lora = a_decay_up.shape[1] + a_aaa_up.shape[1]
    v_first = None
    for i in range(n_a):
        w = dict(norm_g=_row(a_norm1_g[i]), w_in=a_w_in[i].astype(BF16), mu=_row(a_shift_mu[i]),
                 decay_up=_pad_rows(a_decay_up[i], lora, 0).astype(BF16),
                 decay_bias=_row(a_decay_bias[i]),
                 aaa_up=_pad_rows(a_aaa_up[i], lora, a_decay_up.shape[1]).astype(BF16),
                 aaa_bias=_row(a_aaa_bias[i]), gate_up=a_gate_up[i].astype(BF16),
                 k_k=_row(a_k_k[i]), k_a=_row(a_k_a[i]), head_ones=head_ones)
        if i > 0:
            w.update(vres_mu=_row(vres_mu[i - 1]),
                     vres_down=jnp.pad(vres_down[i - 1], ((0, 0), (0, 128 - vres_down.shape[2]))).astype(BF16),
                     vres_up=_pad_rows(vres_up[i - 1], 128, 0).astype(BF16),
                     vres_bias=_row(vres_bias[i - 1]))
        r, lw, k, v, a, b, gate, qmem = _a_pre(x, v_first if i > 0 else None, w, tm)
        if i == 0:
            v_first = v
        y_mix = _scan(r, lw, k, v, a, b, gate, _row(a_lnx_g[i]), _row(a_lnx_b[i]),
                      _row(a_r_k[i]), tc)
        wp = dict(w_out=a_w_out[i].astype(BF16), norm2_g=_row(a_norm2_g[i]),
                  ffn_gu=a_ffn_gu[i].astype(BF16), ffn_down=a_ffn_down[i].astype(BF16))
        x = _post(x, y_mix, qmem, memkv, i, wp, None, tm)

    lat = kv_latent_g.shape[0]
    heads = b_q_up.shape[2] // (QK_NOPE + QK_ROPE)
    per_kv = kv_w_up.shape[1] // heads
    wk_blocks, wv_blocks = [], []
    for hd in range(heads):
        blk = kv_w_up[:, hd * per_kv:(hd + 1) * per_kv]
        wk_blocks.append(jnp.pad(blk[:, :QK_NOPE], ((0, 0), (0, MLA_PAD - QK_NOPE))))
        wv_blocks.append(blk[:, QK_NOPE:])
    wkv = dict(kv_norm_g=_row(kv_norm_g),
               kv_wd=jnp.concatenate([kv_w_down[:, :lat], _rope_cols(kv_w_down[:, lat:], False),
                                      _rope_cols(kv_w_down[:, lat:], True)], axis=1).astype(BF16),
               kv_latent_g=_row(kv_latent_g),
               kv_wk=jnp.concatenate(wk_blocks, axis=1).astype(BF16),
               kv_wv=jnp.concatenate(wv_blocks, axis=1).astype(BF16))
    k_all, v_all = _kv(x, cos_t, sin_t, wkv, tm)

    for j in range(n_b):
        wq = dict(norm_g=_row(b_norm1_g[j]), w_in=b_w_in[j].astype(BF16),
                  q_norm_g=_row(b_q_norm_g[j]),
                  q_up_a=_q_up_padded(b_q_up[j], heads, False),
                  q_up_b=_q_up_padded(b_q_up[j], heads, True))
        q, qmem = _b_pre(x, cos_t, sin_t, wq, tm)
        o = _flash(q, k_all, v_all, tq)
        wp = dict(w_out=b_w_out[j].astype(BF16), norm2_g=_row(b_norm2_g[j]),
                  ffn_gu=b_ffn_gu[j].astype(BF16), ffn_down=b_ffn_down[j].astype(BF16))
        x = _post(x, o, qmem, memkv, n_a + j, wp,
                  _row(final_norm_g) if j == n_b - 1 else None, tm)
    return x
```

```python
import functools

import numpy as np
import jax
import jax.numpy as jnp
from jax import lax
from jax.experimental import pallas as pl
from jax.experimental.pallas import tpu as pltpu

F32 = jnp.float32
BF16 = jnp.bfloat16

HEAD = 64
CHUNK = 64
GROUP = 4
MXU_TILE = 256
GL = GROUP * HEAD
NORM_EPS = 1e-6
LNX_EPS = 64e-5
QK_NOPE = 64
QK_ROPE = 32
MLA_PAD = 128
VMEM_LIMIT = 56 * 1024 * 1024
TILES = dict(tm=512, tc=512, tq=1024)


def _dot(a, b):
    return jnp.dot(a.astype(BF16), b.astype(BF16), preferred_element_type=F32)


def _dot_nt(a, b):
    return lax.dot_general(a.astype(BF16), b.astype(BF16), (((1,), (1,)), ((), ())),
                           preferred_element_type=F32)


def _dot_tn(a, b):
    return lax.dot_general(a.astype(BF16), b.astype(BF16), (((0,), (0,)), ((), ())),
                           preferred_element_type=F32)


def _split(x, terms):
    out = []
    for _ in range(terms):
        p = x.astype(BF16)
        out.append(p)
        x = x - p.astype(F32)
    return out


def _dot_ones_lhs(ones_bf16, x, terms=3):
    acc = None
    for p in _split(x, terms):
        d = jnp.dot(ones_bf16, p, preferred_element_type=F32)
        acc = d if acc is None else acc + d
    return acc


def _rms(x, g, eps=NORM_EPS):
    ms = jnp.mean(x * x, axis=-1, keepdims=True)
    return x * lax.rsqrt(ms + eps) * g


def _sigmoid(x):
    return 1.0 / (1.0 + jnp.exp(-x))


def _shift_rows(x, carry_ref):
    tm = x.shape[0]
    first = lax.broadcasted_iota(jnp.int32, (tm, 1), 0) == 0
    prev = jnp.where(first, carry_ref[0:1, :], pltpu.roll(x, 1, 0))
    carry_ref[0:1, :] = x[tm - 1:tm, :]
    return prev


def _const_spec(shape):
    nd = len(shape)
    return pl.BlockSpec(shape, lambda *_: (0,) * nd, pipeline_mode=pl.Buffered(1))


def _params(sem):
    return pltpu.CompilerParams(dimension_semantics=sem, vmem_limit_bytes=VMEM_LIMIT)


def _tile(n, pref):
    t = min(n, pref)
    assert n % t == 0, (n, t)
    return t


def _memkv_kernel(mem_ref, g_ref, w_ref, o_ref):
    mn = _rms(mem_ref[0], g_ref[...])
    o_ref[0] = _dot(mn, w_ref[...]).astype(o_ref.dtype)


def _memkv(mem, g, w_all):
    bsz, n_mem, d = mem.shape
    n = w_all.shape[1]
    return pl.pallas_call(
        _memkv_kernel,
        grid=(bsz,),
        in_specs=[pl.BlockSpec((1, n_mem, d), lambda b: (b, 0, 0)),
                  _const_spec((1, d)), _const_spec((d, n))],
        out_specs=pl.BlockSpec((1, n_mem, n), lambda b: (b, 0, 0)),
        out_shape=jax.ShapeDtypeStruct((bsz, n_mem, n), BF16),
        compiler_params=_params(("parallel",)),
        name="memkv",
    )(mem, g, w_all)


def _a_pre_kernel(has_vres, rdim, rwkv_in, lora, *refs):
    it = iter(refs)
    x_ref = next(it)
    vfirst_ref = next(it) if has_vres else None
    g_ref, w_in_ref, mu_ref = next(it), next(it), next(it)
    dup_ref, dbias_ref, aup_ref, abias_ref, gup_ref = (next(it) for _ in range(5))
    kk_ref, ka_ref, bd_ref = next(it), next(it), next(it)
    if has_vres:
        vmu_ref, vdown_ref, vup_ref, vbias_ref = (next(it) for _ in range(4))
    r_ref, lw_ref, k_ref, v_ref, a_ref, b_ref, gate_ref, qmem_ref = (next(it) for _ in range(8))
    carry_p = next(it)
    carry_h = next(it) if has_vres else None

    @pl.when(pl.program_id(1) == 0)
    def _():
        carry_p[...] = jnp.zeros_like(carry_p)
        if has_vres:
            carry_h[...] = jnp.zeros_like(carry_h)

    h = _rms(x_ref[0], g_ref[...])
    proj = _dot(h, w_in_ref[...])
    qmem_ref[0] = proj[:, rwkv_in:].astype(qmem_ref.dtype)
    p = proj[:, :rwkv_in]
    pm = p + (_shift_rows(p, carry_p) - p) * mu_ref[...]

    r = pm[:, 0:rdim]
    k = pm[:, rdim:2 * rdim]
    v = pm[:, 2 * rdim:3 * rdim]
    lo = pm[:, 3 * rdim:3 * rdim + lora]
    g_lo = pm[:, 3 * rdim + lora:]

    z = dbias_ref[...] + _dot(jnp.tanh(lo), dup_ref[...])
    nz = -z
    softplus = jnp.maximum(nz, 0.0) + jnp.log1p(jnp.exp(-jnp.abs(nz)))
    log_w = -softplus - 0.5
    lw_ref[0] = -jnp.exp(log_w)
    lr = _sigmoid(abias_ref[...] + _dot(lo, aup_ref[...]))
    gate_ref[0] = _dot(_sigmoid(g_lo), gup_ref[...]).astype(gate_ref.dtype)

    if has_vres:
        hv = h + (_shift_rows(h, carry_h) - h) * vmu_ref[...]
        t = _dot(hv, vdown_ref[...])
        sg = _sigmoid(vbias_ref[...] + _dot(t, vup_ref[...]))
        v = v + (vfirst_ref[0].astype(F32) - v) * sg

    kkr = k * kk_ref[...]
    sq = kkr * kkr
    norm = jnp.sqrt(jnp.concatenate(
        [_dot(sq[:, g:g + GL], bd_ref[...]) for g in range(0, rdim, GL)], axis=1))
    kk = kkr / jnp.maximum(norm, 1e-12)
    r_ref[0] = r.astype(r_ref.dtype)
    k_ref[0] = (k * (1.0 + (lr - 1.0) * ka_ref[...])).astype(k_ref.dtype)
    v_ref[0] = v.astype(v_ref.dtype)
    a_ref[0] = (-kk).astype(a_ref.dtype)
    b_ref[0] = (kk * lr).astype(b_ref.dtype)


def _a_pre(x, vfirst, w, tm):
    bsz, s_len, d = x.shape
    rdim = w["k_k"].shape[1]
    n_in = w["w_in"].shape[1]
    rwkv_in = w["mu"].shape[1]
    mem_dim = n_in - rwkv_in
    has_vres = vfirst is not None
    tok = lambda n: pl.BlockSpec((1, tm, n), lambda b, j: (b, j, 0))
    args, specs = [x], [tok(d)]
    if has_vres:
        args.append(vfirst)
        specs.append(tok(rdim))
    names = ["norm_g", "w_in", "mu", "decay_up", "decay_bias", "aaa_up", "aaa_bias", "gate_up",
             "k_k", "k_a", "head_ones"]
    if has_vres:
        names += ["vres_mu", "vres_down", "vres_up", "vres_bias"]
    for n in names:
        args.append(w[n])
        specs.append(_const_spec(w[n].shape))
    scratch = [pltpu.VMEM((8, rwkv_in), F32)]
    if has_vres:
        scratch.append(pltpu.VMEM((8, d), F32))
    out_shape = [jax.ShapeDtypeStruct((bsz, s_len, rdim), F32 if n == 1 else BF16) for n in range(7)] + \
                [jax.ShapeDtypeStruct((bsz, s_len, mem_dim), BF16)]
    out_specs = [tok(rdim)] * 7 + [tok(mem_dim)]
    return pl.pallas_call(
        functools.partial(_a_pre_kernel, has_vres, rdim, rwkv_in, w["decay_up"].shape[0]),
        grid=(bsz, s_len // tm),
        in_specs=specs, out_specs=out_specs, out_shape=out_shape,
        scratch_shapes=scratch,
        compiler_params=_params(("parallel", "arbitrary")),
        name="a_pre_vres" if has_vres else "a_pre",
    )(*args)


def _stack_heads(x_bf16, hm_ref):
    return jnp.concatenate([x_bf16 * hm_ref[h] for h in range(GROUP)], axis=0)


def _scan_prepare(ins, consts, done):
    tri, hm_ref, mask2, icat, bd = consts
    c = CHUNK
    gc = GROUP * c
    each = range(len(ins))
    bd_b = bd[...]
    r, lw, k, a, b = zip(*ins)

    cl = [_dot_ones_lhs(tri[...], lw[g]) for g in each]
    yield
    w_end = [jnp.exp(cl[g][c - 1:c, :]) for g in each]
    e_inc = [jnp.exp(cl[g]) for g in each]
    e_exc = [jnp.exp(cl[g] - lw[g]) for g in each]
    e_inv = [jnp.exp(-cl[g]) for g in each]
    kt = [k[g] * e_inv[g] for g in each]
    bt = [b[g] * e_inv[g] for g in each]
    ar = [jnp.concatenate([a[g] * e_exc[g], r[g] * e_inc[g]], axis=0).astype(BF16) for g in each]
    bks = [jnp.concatenate([_stack_heads(bt[g].astype(BF16), hm_ref),
                            _stack_heads(kt[g].astype(BF16), hm_ref)], axis=0) for g in each]
    m = [_dot_nt(ar[g], bks[g]) * mask2[...] for g in each]
    l_ab = [m[g][:c, :gc] for g in each]
    bkh = [jnp.concatenate([bt[g] * w_end[g], kt[g] * w_end[g]], axis=0).astype(BF16) for g in each]
    yield

    def blockdiag(p_bf16):
        return jnp.concatenate([p_bf16] * GROUP, axis=0) * bd_b

    pw = [l_ab[g].astype(BF16) for g in each]
    pw = [jnp.dot(pw[g], blockdiag(pw[g]), preferred_element_type=F32) for g in each]
    inv = [icat[...] + l_ab[g] for g in each]
    yield
    steps = int(np.log2(c)) - 1
    for it in range(steps):
        pbd = [blockdiag(pw[g].astype(BF16)) for g in each]
        if it + 1 < steps:
            both = [jnp.dot(jnp.concatenate([inv[g], pw[g]], axis=0).astype(BF16), pbd[g],
                            preferred_element_type=F32) for g in each]
            inv = [inv[g] + both[g][:c] for g in each]
            pw = [both[g][c:] for g in each]
        else:
            inv = [inv[g] + jnp.dot(inv[g].astype(BF16), pbd[g], preferred_element_type=F32)
                   for g in each]
        yield
    done(ar, [m[g][:c, gc:].astype(BF16) for g in each], [m[g][c:, :].astype(BF16) for g in each],
         [inv[g].astype(BF16) for g in each], bkh, w_end)


def _scan_advance(pre, vb, states, hm_ref, bd_f, done):
    c = CHUNK
    each = range(len(states))
    ar, l_ak, m_r, inv, bkh, w_end = pre
    ars = [_dot_nt(ar[g], states[g]) for g in each]
    vs = [_stack_heads(vb[g], hm_ref) for g in each]
    z = [ars[g][:c] + jnp.dot(l_ak[g], vs[g], preferred_element_type=F32) for g in each]
    yield
    u = [jnp.dot(inv[g], _stack_heads(z[g].astype(BF16), hm_ref), preferred_element_type=F32)
         for g in each]
    ub = [u[g].astype(BF16) for g in each]
    yield
    y = [ars[g][c:] + jnp.dot(m_r[g], jnp.concatenate([_stack_heads(ub[g], hm_ref), vs[g]], axis=0),
                              preferred_element_type=F32) for g in each]
    s_new = [states[g] * w_end[g]
             + bd_f * _dot_tn(jnp.concatenate([ub[g], vb[g]], axis=0), bkh[g]) for g in each]
    done(y, s_new)


def _weave(main, *others):
    live = [main, *others]
    while live:
        for g in list(live):
            if next(g, live) is live:
                live.remove(g)


def _scan_kernel(n_groups, n_chunks, r_ref, lw_ref, k_ref, v_ref, a_ref, b_ref, gate_ref,
                 lnxg_ref, lnxb_ref, rk_ref, tri_ref, hm_ref, mask2_ref, icat_ref, bd_ref,
                 o_ref, state_ref, ar_scr, lak_scr, mr_scr, inv_scr, bkh_scr, wend_scr, y_scr):
    @pl.when(pl.program_id(1) == 0)
    def _():
        state_ref[...] = jnp.zeros_like(state_ref)

    c = CHUNK
    gc = GROUP * c
    consts = (tri_ref, hm_ref, mask2_ref, icat_ref, bd_ref)
    seqs = [(bi, g) for bi in range(r_ref.shape[0]) for g in range(n_groups)]
    lanes = [slice(g * GL, (g + 1) * GL) for g in range(n_groups)]
    cat = [slice(g * gc, (g + 1) * gc) for g in range(n_groups)]
    cat2 = [slice(g * 2 * gc, (g + 1) * 2 * gc) for g in range(n_groups)]
    bd_b = bd_ref[...]
    bd_f = bd_b.astype(F32)

    def rows(ci, n=c):
        return slice(ci * n, (ci + 1) * n)

    def prepare(ci):
        ins = [tuple(ref[bi, rows(ci), lanes[g]].astype(F32)
                     for ref in (r_ref, lw_ref, k_ref, a_ref, b_ref)) for bi, g in seqs]

        def done(ar, l_ak, m_r, inv, bkh, w_end):
            for n, (bi, g) in enumerate(seqs):
                ar_scr[bi, rows(ci, 2 * c), lanes[g]] = ar[n]
                lak_scr[bi, rows(ci), cat[g]] = l_ak[n]
                mr_scr[bi, rows(ci), cat2[g]] = m_r[n]
                inv_scr[bi, rows(ci), cat[g]] = inv[n]
                bkh_scr[bi, rows(ci, 2 * c), lanes[g]] = bkh[n]
                wend_scr[bi, ci, :, lanes[g]] = w_end[n]

        return _scan_prepare(ins, consts, done)

    def advance(ci):
        pre = ([ar_scr[bi, rows(ci, 2 * c), lanes[g]] for bi, g in seqs],
               [lak_scr[bi, rows(ci), cat[g]] for bi, g in seqs],
               [mr_scr[bi, rows(ci), cat2[g]] for bi, g in seqs],
               [inv_scr[bi, rows(ci), cat[g]] for bi, g in seqs],
               [bkh_scr[bi, rows(ci, 2 * c), lanes[g]] for bi, g in seqs],
               [wend_scr[bi, ci, :, lanes[g]] for bi, g in seqs])
        vb = [v_ref[bi, rows(ci), lanes[g]].astype(BF16) for bi, g in seqs]

        def done(y, s_new):
            for n, (bi, g) in enumerate(seqs):
                state_ref[n] = s_new[n]
                y_scr[bi, rows(ci), lanes[g]] = y[n]

        return _scan_advance(pre, vb, [state_ref[n] for n in range(len(seqs))], hm_ref, bd_f, done)

    _weave(prepare(0))
    for ci in range(n_chunks):
        _weave(advance(ci), *([prepare(ci + 1)] if ci + 1 < n_chunks else []))

    bb, tc = y_scr.shape[0], y_scr.shape[1]
    flat = lambda t: t.reshape(bb * tc, GL).astype(F32)
    for ln in lanes:
        y = flat(y_scr[:, :, ln])
        mean = _dot(y, bd_b) * (1.0 / HEAD)
        d = y - mean
        var = _dot(d * d, bd_b) * (1.0 / HEAD)
        v = flat(v_ref[:, :, ln])
        bonus = _dot(flat(r_ref[:, :, ln]) * flat(k_ref[:, :, ln]) * rk_ref[:, ln], bd_b) * v
        gn = d * lax.rsqrt(var + LNX_EPS) * lnxg_ref[:, ln] + lnxb_ref[:, ln]
        out = (gn + bonus) * flat(gate_ref[:, :, ln])
        o_ref[:, :, ln] = out.reshape(bb, tc, GL).astype(o_ref.dtype)


def _scan_consts():
    c, g = CHUNK, GROUP
    gc = g * c
    i = np.arange(c)[:, None]
    j = np.arange(gc)[None, :] % c
    strict = (j < i).astype(np.float32)
    incl = (j <= i).astype(np.float32)
    mask2 = np.concatenate([np.concatenate([strict, strict], 1),
                            np.concatenate([incl, incl], 1)], 0)
    icat = (j == i).astype(np.float32)
    tri = (np.arange(c)[None, :] <= np.arange(c)[:, None]).astype(np.float32)
    lane_head = np.arange(GL) // HEAD
    hm = (lane_head[None, None, :] == np.arange(g)[:, None, None]).astype(np.float32)
    bd = (lane_head[:, None] == lane_head[None, :]).astype(np.float32)
    return (jnp.asarray(tri, BF16), jnp.asarray(hm, BF16), jnp.asarray(mask2, F32),
            jnp.asarray(icat, F32), jnp.asarray(bd, BF16))


def _scan(r, lw, k, v, a, b, gate, lnx_g, lnx_b, r_k, rows_per_step):
    bsz, s_len, rdim = r.shape
    n_groups = rdim // GL
    bb = next(n for n in (4, 2, 1) if bsz % n == 0)
    tc = max(CHUNK, rows_per_step // bb)
    assert s_len % tc == 0
    n_chunks = tc // CHUNK
    gc = GROUP * CHUNK
    tok = pl.BlockSpec((bb, tc, rdim), lambda bi, j: (bi, j, 0))
    consts = _scan_consts()
    small = [lnx_g, lnx_b, r_k] + list(consts)
    return pl.pallas_call(
        functools.partial(_scan_kernel, n_groups, n_chunks),
        grid=(bsz // bb, s_len // tc),
        in_specs=[tok] * 7 + [_const_spec(t.shape) for t in small],
        out_specs=tok,
        out_shape=jax.ShapeDtypeStruct((bsz, s_len, rdim), BF16),
        scratch_shapes=[pltpu.VMEM((bb * n_groups, GL, GL), F32),
                        pltpu.VMEM((bb, 2 * tc, rdim), BF16),
                        pltpu.VMEM((bb, tc, n_groups * gc), BF16),
                        pltpu.VMEM((bb, tc, n_groups * 2 * gc), BF16),
                        pltpu.VMEM((bb, tc, n_groups * gc), BF16),
                        pltpu.VMEM((bb, 2 * tc, rdim), BF16),
                        pltpu.VMEM((bb, n_chunks, 1, rdim), F32),
                        pltpu.VMEM((bb, tc, rdim), F32)],
        compiler_params=_params(("parallel", "arbitrary")),
        name="rwkv_scan",
    )(r, lw, k, v, a, b, gate, *small)


def _ffn_slices(fh):
    n = fh // MXU_TILE
    if fh % MXU_TILE or n < 2:
        return ((0, fh),)
    mid = (n + 1) // 2 * MXU_TILE
    return ((0, mid), (mid, fh))


def _post_kernel(mem_heads, final, x_ref, mix_ref, qmem_ref, mk_ref, mv_ref, hm_ref,
                 w_out_ref, g2_ref, w_gu_ref, w_dn_ref, gf_ref, o_ref):
    mix_dim = mix_ref.shape[2]
    fh = w_dn_ref.shape[0]

    qm = (qmem_ref[0] * (HEAD ** -0.5)).astype(BF16)
    mk = mk_ref[0]
    mv = mv_ref[0]
    m = None
    for hd in range(mem_heads):
        s = _dot_nt(qm * hm_ref[hd], mk)
        s = s - jnp.max(s, axis=-1, keepdims=True)
        e = jnp.exp(s)
        p = e / jnp.sum(e, axis=-1, keepdims=True)
        o = _dot(p, mv) * hm_ref[hd].astype(F32)
        m = o if m is None else m + o

    x1 = x_ref[0] + _dot(mix_ref[0], w_out_ref[0:mix_dim, :]) + _dot(m, w_out_ref[mix_dim:, :])
    h2 = _rms(x1, g2_ref[...]).astype(BF16)
    acc = None
    for h0, h1 in _ffn_slices(fh):
        gt = jnp.dot(h2, w_gu_ref[:, h0:h1], preferred_element_type=F32)
        ut = jnp.dot(h2, w_gu_ref[:, fh + h0:fh + h1], preferred_element_type=F32)
        act = gt * _sigmoid(gt) * ut
        dt = _dot(act, w_dn_ref[h0:h1, :])
        acc = dt if acc is None else acc + dt
    x2 = x1 + acc
    if final:
        x2 = _rms(x2, gf_ref[...])
    o_ref[0] = x2


def _post(x, mix, qmem, memkv, layer, w, final_g, tm):
    bsz, s_len, d = x.shape
    mix_dim = mix.shape[2]
    mem_dim = qmem.shape[2]
    n_mem = memkv.shape[1]
    mem_heads = mem_dim // HEAD
    lane_head = np.arange(mem_dim) // HEAD
    hm = jnp.asarray((lane_head[None, None, :] == np.arange(mem_heads)[:, None, None]), BF16)
    tok = lambda n: pl.BlockSpec((1, tm, n), lambda b, j: (b, j, 0))
    final = final_g is not None
    gf = final_g if final else w["norm2_g"]
    return pl.pallas_call(
        functools.partial(_post_kernel, mem_heads, final),
        grid=(bsz, s_len // tm),
        in_specs=[tok(d), tok(mix_dim), tok(mem_dim),
                  pl.BlockSpec((1, n_mem, mem_dim), lambda b, j: (b, 0, 2 * layer)),
                  pl.BlockSpec((1, n_mem, mem_dim), lambda b, j: (b, 0, 2 * layer + 1)),
                  _const_spec(hm.shape), _const_spec(w["w_out"].shape),
                  _const_spec(w["norm2_g"].shape), _const_spec(w["ffn_gu"].shape),
                  _const_spec(w["ffn_down"].shape), _const_spec(gf.shape)],
        out_specs=tok(d),
        out_shape=jax.ShapeDtypeStruct((bsz, s_len, d), F32),
        compiler_params=_params(("parallel", "parallel")),
        name="post_final" if final else "post",
    )(x, mix, qmem, memkv, memkv, hm, w["w_out"], w["norm2_g"], w["ffn_gu"], w["ffn_down"], gf)


def _kv_kernel(lat, x_ref, cos_ref, sin_ref, g_ref, wd_ref, lg_ref, wk_ref, wv_ref, k_ref, v_ref):
    hk = _rms(x_ref[0], g_ref[...])
    ckr = _dot(hk, wd_ref[...])
    ckv = _rms(ckr[:, :lat], lg_ref[...])
    krot = ckr[:, lat:lat + MLA_PAD] * cos_ref[0] + ckr[:, lat + MLA_PAD:] * sin_ref[0]
    kn = _dot(ckv, wk_ref[...])
    heads = kn.shape[1] // MLA_PAD
    k_ref[0] = (kn + jnp.concatenate([krot] * heads, axis=1)).astype(k_ref.dtype)
    v_ref[0] = _dot(ckv, wv_ref[...]).astype(v_ref.dtype)


def _kv(x, cos_t, sin_t, w, tm):
    bsz, s_len, d = x.shape
    lat = w["kv_latent_g"].shape[1]
    nk = w["kv_wk"].shape[1]
    nv = w["kv_wv"].shape[1]
    tok = lambda n: pl.BlockSpec((1, tm, n), lambda b, j: (b, j, 0))
    names = ["kv_norm_g", "kv_wd", "kv_latent_g", "kv_wk", "kv_wv"]
    return pl.pallas_call(
        functools.partial(_kv_kernel, lat),
        grid=(bsz, s_len // tm),
        in_specs=[tok(d), tok(MLA_PAD), tok(MLA_PAD)] + [_const_spec(w[n].shape) for n in names],
        out_specs=[tok(nk), tok(nv)],
        out_shape=[jax.ShapeDtypeStruct((bsz, s_len, nk), BF16),
                   jax.ShapeDtypeStruct((bsz, s_len, nv), BF16)],
        compiler_params=_params(("parallel", "parallel")),
        name="mla_kv",
    )(x, cos_t, sin_t, *[w[n] for n in names])


def _b_pre_kernel(q_lora, scale, x_ref, cos_ref, sin_ref, g_ref, w_in_ref, qg_ref, qa_ref, qb_ref,
                  q_ref, qmem_ref):
    h = _rms(x_ref[0], g_ref[...])
    proj = _dot(h, w_in_ref[...])
    qmem_ref[0] = proj[:, q_lora:].astype(qmem_ref.dtype)
    cq = _rms(proj[:, :q_lora], qg_ref[...]).astype(BF16)
    qa = jnp.dot(cq, qa_ref[...], preferred_element_type=F32)
    qb = jnp.dot(cq, qb_ref[...], preferred_element_type=F32)
    heads = qa.shape[1] // MLA_PAD
    cos_t = jnp.concatenate([cos_ref[0]] * heads, axis=1)
    sin_t = jnp.concatenate([sin_ref[0]] * heads, axis=1)
    q_ref[0] = ((qa * cos_t + qb * sin_t) * scale).astype(q_ref.dtype)


def _b_pre(x, cos_t, sin_t, w, tm):
    bsz, s_len, d = x.shape
    q_lora = w["q_norm_g"].shape[1]
    mem_dim = w["w_in"].shape[1] - q_lora
    nq = w["q_up_a"].shape[1]
    scale = (QK_NOPE + QK_ROPE) ** -0.5 * float(np.log2(np.e))
    tok = lambda n: pl.BlockSpec((1, tm, n), lambda b, j: (b, j, 0))
    names = ["norm_g", "w_in", "q_norm_g", "q_up_a", "q_up_b"]
    return pl.pallas_call(
        functools.partial(_b_pre_kernel, q_lora, scale),
        grid=(bsz, s_len // tm),
        in_specs=[tok(d), tok(MLA_PAD), tok(MLA_PAD)] + [_const_spec(w[n].shape) for n in names],
        out_specs=[tok(nq), tok(mem_dim)],
        out_shape=[jax.ShapeDtypeStruct((bsz, s_len, nq), BF16),
                   jax.ShapeDtypeStruct((bsz, s_len, mem_dim), BF16)],
        compiler_params=_params(("parallel", "parallel")),
        name="mla_q",
    )(x, cos_t, sin_t, *[w[n] for n in names])


def _flash_kernel(t, q_ref, k_ref, v_ref, o_ref, m_ref, l_ref, acc_ref, sa_ref, sb_ref):
    i = pl.program_id(2)
    half = t // 2
    split = half % 128 == 0
    m_ref[...] = jnp.full_like(m_ref, -jnp.inf)
    l_ref[...] = jnp.zeros_like(l_ref)
    acc_ref[...] = jnp.zeros_like(acc_ref)
    low = lax.broadcasted_iota(jnp.int32, (1, 2 * HEAD), 1) < HEAD
    diag_blocks = [(0, half, half), (half, t, t)] if split else [(0, t, t)]

    def key_rows(j):
        return pl.ds(pl.multiple_of(j * t, t), t)

    def scores(j, s_ref, blocks):
        for hd in range(2):
            for r0, r1, nc in blocks:
                q = q_ref[0, r0:r1, hd * MLA_PAD:(hd + 1) * MLA_PAD]
                k = k_ref[0, pl.ds(pl.multiple_of(j * t, t), nc), hd * MLA_PAD:(hd + 1) * MLA_PAD]
                s_ref[hd, r0:r1, 0:nc] = lax.dot_general(q, k, (((1,), (1,)), ((), ())),
                                                        preferred_element_type=F32)

    def apply(j, s_ref, blocks, diagonal):
        vpair = v_ref[0, key_rows(j), :]
        vhead = (jnp.where(low, vpair, 0), jnp.where(low, 0, vpair))
        for r0, r1, nc in blocks:
            nr = r1 - r0
            if diagonal:
                visible = (lax.broadcasted_iota(jnp.int32, (nr, nc), 0) + r0
                           >= lax.broadcasted_iota(jnp.int32, (nr, nc), 1))
            alphas, pvs = [], []
            for hd in range(2):
                s = s_ref[hd, r0:r1, 0:nc]
                if diagonal:
                    s = jnp.where(visible, s, -jnp.inf)
                m_prev = m_ref[hd, r0:r1, :]
                m_new = jnp.maximum(m_prev, jnp.max(s, axis=-1, keepdims=True))
                alpha = jnp.exp2(m_prev - m_new)
                p = jnp.exp2(s - jnp.concatenate([m_new] * (nc // 128), axis=1))
                part = p[:, 0:128]
                for c in range(1, nc // 128):
                    part = part + p[:, c * 128:(c + 1) * 128]
                l_ref[hd, r0:r1, :] = alpha * l_ref[hd, r0:r1, :] + part
                m_ref[hd, r0:r1, :] = m_new
                pvs.append(jnp.dot(p.astype(BF16), vhead[hd][0:nc], preferred_element_type=F32))
                alphas.append(alpha)
            acc_ref[r0:r1, :] = (acc_ref[r0:r1, :] * jnp.where(low, alphas[0], alphas[1])
                                 + pvs[0] + pvs[1])

    full = [(0, t, t)]

    scores(0, sa_ref, full)

    def two_tiles(n, carry):
        scores(2 * n + 1, sb_ref, full)
        apply(2 * n, sa_ref, full, False)

        scores(2 * n + 2, sa_ref, full)
        apply(2 * n + 1, sb_ref, full, False)
        return carry

    lax.fori_loop(0, i // 2, two_tiles, 0)

    @pl.when(i % 2 == 1)
    def _():
        scores(i, sb_ref, diag_blocks)
        apply(i - 1, sa_ref, full, False)
        apply(i, sb_ref, diag_blocks, True)

    @pl.when(i % 2 == 0)
    def _():
        apply(i, sa_ref, diag_blocks, True)

    l0 = jnp.sum(l_ref[0], axis=-1, keepdims=True)
    l1 = jnp.sum(l_ref[1], axis=-1, keepdims=True)
    o_ref[0] = (acc_ref[...] / jnp.where(low, l0, l1)).astype(o_ref.dtype)


def _flash(q, k, v, t):
    bsz, s_len, nq = q.shape
    pairs = nq // (2 * MLA_PAD)
    return pl.pallas_call(
        functools.partial(_flash_kernel, t),
        grid=(bsz, pairs, s_len // t),
        in_specs=[pl.BlockSpec((1, t, 2 * MLA_PAD), lambda b, p, i: (b, i, p)),
                  pl.BlockSpec((1, s_len, 2 * MLA_PAD), lambda b, p, i: (b, 0, p)),
                  pl.BlockSpec((1, s_len, 2 * HEAD), lambda b, p, i: (b, 0, p))],
        out_specs=pl.BlockSpec((1, t, 2 * HEAD), lambda b, p, i: (b, i, p)),
        out_shape=jax.ShapeDtypeStruct((bsz, s_len, pairs * 2 * HEAD), BF16),
        scratch_shapes=[pltpu.VMEM((2, t, 128), F32), pltpu.VMEM((2, t, 128), F32),
                        pltpu.VMEM((t, 2 * HEAD), F32),
                        pltpu.VMEM((2, t, t), F32), pltpu.VMEM((2, t, t), F32)],
        compiler_params=_params(("parallel", "parallel", "arbitrary")),
        name="mla_flash",
    )(q, k, v)


def _row(v):
    return v.reshape(1, -1).astype(F32)


def _pad_rows(w, total, offset):
    out = jnp.zeros((total, w.shape[1]), w.dtype)
    return out.at[offset:offset + w.shape[0]].set(w)


def _block_ones(n):
    h = np.arange(n) // HEAD
    return jnp.asarray(h[:, None] == h[None, :], BF16)


def _rope_cols(w_rope, swap):
    half = QK_ROPE // 2
    t1, t2 = w_rope[:, :half], w_rope[:, half:]
    if swap:
        t1, t2 = t2, t1
    z = lambda n: jnp.zeros((w_rope.shape[0], n), w_rope.dtype)
    return jnp.concatenate([z(QK_NOPE), t1, t2, z(MLA_PAD - QK_NOPE - QK_ROPE)], axis=1)


def _q_up_padded(q_up, heads, swap):
    per = QK_NOPE + QK_ROPE
    blocks = []
    for hd in range(heads):
        wh = q_up[:, hd * per:(hd + 1) * per]
        rope = _rope_cols(wh[:, QK_NOPE:], swap)
        if swap:
            blocks.append(rope)
        else:
            blocks.append(rope.at[:, :QK_NOPE].set(wh[:, :QK_NOPE]))
    return jnp.concatenate(blocks, axis=1).astype(BF16)


def kernel(x, mem, positions, mem_norm_g, a_norm1_g, a_w_in, a_shift_mu, a_decay_up, a_decay_bias, a_aaa_up, a_aaa_bias, a_gate_up, a_k_k, a_k_a, a_r_k, a_lnx_g, a_lnx_b, a_mem_kv, a_w_out, a_norm2_g, a_ffn_gu, a_ffn_down, vres_mu, vres_down, vres_up, vres_bias, kv_norm_g, kv_w_down, kv_latent_g, kv_w_up, b_norm1_g, b_w_in, b_q_norm_g, b_q_up, b_mem_kv, b_w_out, b_norm2_g, b_ffn_gu, b_ffn_down, final_norm_g):
    bsz, s_len, d = x.shape
    n_a = a_w_in.shape[0]
    n_b = b_w_in.shape[0]
    rdim = a_k_k.shape[1]
    assert CHUNK == HEAD and rdim % GL == 0 and s_len % CHUNK == 0
    tm = _tile(s_len, TILES["tm"])
    tc = _tile(s_len, TILES["tc"])
    tq = _tile(s_len, TILES["tq"])

    half = QK_ROPE // 2
    inv_freq = 10000.0 ** (-jnp.arange(half, dtype=F32) / half)
    ang = positions.astype(F32)[..., None] * inv_freq
    cos, sin = lax.optimization_barrier((jnp.cos(ang), jnp.sin(ang)))
    ones = jnp.ones((bsz, s_len, QK_NOPE), F32)
    zpad = jnp.zeros((bsz, s_len, MLA_PAD - QK_NOPE - QK_ROPE), F32)
    cos_t = jnp.concatenate([ones, cos, cos, zpad], axis=-1)
    sin_t = jnp.concatenate([0 * ones, -sin, sin, zpad], axis=-1)

    memkv = _memkv(mem, _row(mem_norm_g),
                   jnp.concatenate(list(a_mem_kv) + list(b_mem_kv), axis=1).astype(BF16))
    head_ones = _block_ones(GL)

    lora = a_decay_up.shape[1] + a_aaa_up.shape[1]
    v_first = None
    for i in range(n_a):
        w = dict(norm_g=_row(a_norm1_g[i]), w_in=a_w_in[i].astype(BF16), mu=_row(a_shift_mu[i]),
                 decay_up=_pad_rows(a_decay_up[i], lora, 0).astype(BF16),
                 decay_bias=_row(a_decay_bias[i]),
                 aaa_up=_pad_rows(a_aaa_up[i], lora, a_decay_up.shape[1]).astype(BF16),
                 aaa_bias=_row(a_aaa_bias[i]), gate_up=a_gate_up[i].astype(BF16),
                 k_k=_row(a_k_k[i]), k_a=_row(a_k_a[i]), head_ones=head_ones)
        if i > 0:
            w.update(vres_mu=_row(vres_mu[i - 1]),
                     vres_down=jnp.pad(vres_down[i - 1], ((0, 0), (0, 128 - vres_down.shape[2]))).astype(BF16),
                     vres_up=_pad_rows(vres_up[i - 1], 128, 0).astype(BF16),
                     vres_bias=_row(vres_bias[i - 1]))
        r, lw, k, v, a, b, gate, qmem = _a_pre(x, v_first if i > 0 else None, w, tm)
        if i == 0:
            v_first = v
        y_mix = _scan(r, lw, k, v, a, b, gate, _row(a_lnx_g[i]), _row(a_lnx_b[i]),
                      _row(a_r_k[i]), tc)
        wp = dict(w_out=a_w_out[i].astype(BF16), norm2_g=_row(a_norm2_g[i]),
                  ffn_gu=a_ffn_gu[i].astype(BF16), ffn_down=a_ffn_down[i].astype(BF16))
        x = _post(x, y_mix, qmem, memkv, i, wp, None, tm)

    lat = kv_latent_g.shape[0]
    heads = b_q_up.shape[2] // (QK_NOPE + QK_ROPE)
    per_kv = kv_w_up.shape[1] // heads
    wk_blocks, wv_blocks = [], []
    for hd in range(heads):
        blk = kv_w_up[:, hd * per_kv:(hd + 1) * per_kv]
        wk_blocks.append(jnp.pad(blk[:, :QK_NOPE], ((0, 0), (0, MLA_PAD - QK_NOPE))))
        wv_blocks.append(blk[:, QK_NOPE:])
    wkv = dict(kv_norm_g=_row(kv_norm_g),
               kv_wd=jnp.concatenate([kv_w_down[:, :lat], _rope_cols(kv_w_down[:, lat:], False),
                                      _rope_cols(kv_w_down[:, lat:], True)], axis=1).astype(BF16),
               kv_latent_g=_row(kv_latent_g),
               kv_wk=jnp.concatenate(wk_blocks, axis=1).astype(BF16),
               kv_wv=jnp.concatenate(wv_blocks, axis=1).astype(BF16))
    k_all, v_all = _kv(x, cos_t, sin_t, wkv, tm)

    for j in range(n_b):
        wq = dict(norm_g=_row(b_norm1_g[j]), w_in=b_w_in[j].astype(BF16),
                  q_norm_g=_row(b_q_norm_g[j]),
                  q_up_a=_q_up_padded(b_q_up[j], heads, False),
                  q_up_b=_q_up_padded(b_q_up[j], heads, True))
        q, qmem = _b_pre(x, cos_t, sin_t, wq, tm)
        o = _flash(q, k_all, v_all, tq)
        wp = dict(w_out=b_w_out[j].astype(BF16), norm2_g=_row(b_norm2_g[j]),
                  ffn_gu=b_ffn_gu[j].astype(BF16), ffn_down=b_ffn_down[j].astype(BF16))
        x = _post(x, o, qmem, memkv, n_a + j, wp,
                  _row(final_norm_g) if j == n_b - 1 else None, tm)
    return x
```
